```python
import jax, jax.numpy as jnp
from jax import lax
import numpy as np

D_MODEL = 2048
BATCH = 2
SEQ = 16384
DEPTH = 1
DEC_BATCH = 32
DEC_SEQ = 64
PAST_LEN = 1024

CHUNK = 64
N_META = 16
MIX_WIDTH = D_MODEL
HG_WIDTH = MIX_WIDTH // 2
HG_HEAD_DIM = 128
HG_HEADS = HG_WIDTH // HG_HEAD_DIM
RW_WIDTH = MIX_WIDTH - HG_WIDTH
RW_HEAD_DIM = 64
RW_HEADS = RW_WIDTH // RW_HEAD_DIM
RW_LORA = 64
HG_PROJ = 4 * HG_WIDTH
RW_SHIFT_WIDTH = 4 * RW_WIDTH + 2 * RW_LORA
P_TOTAL = HG_PROJ + RW_SHIFT_WIDTH
NORM_EPS = 1e-6
RW_GN_EPS = 64e-5
KK_EPS = 1e-12

kernel_name = "hymba_hgrn2_rwkv7_streaming_step"


def _rmsnorm(x, g):
    xf = x.astype(jnp.float32)
    y = xf * lax.rsqrt(jnp.mean(xf * xf, axis=-1, keepdims=True) + NORM_EPS)
    return (y * g.astype(jnp.float32)).astype(x.dtype)


def _hgrn2_chunk(S, xs):
    q, k, v, g = xs
    G = jnp.cumsum(g, axis=2)
    causal = jnp.tril(jnp.ones((CHUNK, CHUNK), dtype=bool))
    diff = G[:, :, :, None, :] - G[:, :, None, :, :]
    decay = jnp.exp(jnp.where(causal[None, None, :, :, None], diff, -jnp.inf))
    scores = jnp.einsum('bhid,bhjd,bhijd->bhij', q, k, decay)
    o = (jnp.einsum('bhij,bhjv->bhiv', scores, v)
         + jnp.einsum('bhid,bhdv->bhiv', q * jnp.exp(G), S))
    G_last = G[:, :, -1:, :]
    S_new = (jnp.exp(G_last[:, :, 0, :])[..., None] * S
             + jnp.einsum('bhjd,bhjv->bhdv', k * jnp.exp(G_last - G), v))
    return S_new, o


def _hgrn2_mixer(p, lb, norm_g, S0):
    B, T, _ = p.shape
    q, f, i, gate = jnp.split(p, 4, axis=-1)
    q = jax.nn.silu(q.astype(jnp.float32))
    forget = lb + (1.0 - lb) * jax.nn.sigmoid(f.astype(jnp.float32))
    log_f = jnp.log(forget)
    k = 1.0 - forget
    v = i.astype(jnp.float32)
    n_chunks = -(-T // CHUNK)
    pad = n_chunks * CHUNK - T

    def to_chunks(t):
        t = jnp.pad(t, ((0, 0), (0, pad), (0, 0)))
        t = t.reshape(B, n_chunks, CHUNK, HG_HEADS, HG_HEAD_DIM)
        return t.transpose(1, 0, 3, 2, 4)

    S_final, o = lax.scan(_hgrn2_chunk, S0.astype(jnp.float32),
                          (to_chunks(q), to_chunks(k), to_chunks(v), to_chunks(log_f)))
    o = o.transpose(1, 0, 3, 2, 4).reshape(B, n_chunks * CHUNK, HG_HEADS, HG_HEAD_DIM)[:, :T]
    o = o * lax.rsqrt(jnp.mean(o * o, axis=-1, keepdims=True) + NORM_EPS)
    o = o * norm_g.astype(jnp.float32).reshape(HG_HEADS, HG_HEAD_DIM)
    o = o.reshape(B, T, HG_WIDTH) * jax.nn.silu(gate.astype(jnp.float32))
    return o.astype(p.dtype), S_final


def _rwkv7_step(S, inp):
    r_t, w_t, k_t, v_t, kk_t, b_t = inp
    sa = jnp.einsum('bhvk,bhk->bhv', S, -kk_t)
    S = (S * w_t[:, :, None, :] + sa[..., None] * b_t[:, :, None, :]
         + v_t[..., None] * k_t[:, :, None, :])
    return S, jnp.einsum('bhvk,bhk->bhv', S, r_t)


def _rwkv7_mixer(p, shift_prev, mu, w0, w2, a0, a2, k_k, k_a, r_k, ln_w, ln_b, S0):
    B, T, _ = p.shape
    prev = jnp.concatenate([shift_prev.astype(p.dtype), p[:, :-1]], axis=1)
    xm = (p + (prev - p) * mu).astype(jnp.float32)
    r = xm[..., :RW_WIDTH]
    k = xm[..., RW_WIDTH:2 * RW_WIDTH]
    v = xm[..., 2 * RW_WIDTH:3 * RW_WIDTH]
    gate = xm[..., 3 * RW_WIDTH:4 * RW_WIDTH]
    wd = xm[..., 4 * RW_WIDTH:4 * RW_WIDTH + RW_LORA]
    ad = xm[..., 4 * RW_WIDTH + RW_LORA:]
    w = w0.astype(jnp.float32) + jnp.tanh(wd) @ w2.astype(jnp.float32)
    decay = jnp.exp(-jnp.exp(-jax.nn.softplus(-w) - 0.5))
    a = jax.nn.sigmoid(a0.astype(jnp.float32) + ad @ a2.astype(jnp.float32))
    heads = lambda t: t.reshape(B, T, RW_HEADS, RW_HEAD_DIM)
    kk = heads(k * k_k.astype(jnp.float32))
    kk = kk / jnp.maximum(jnp.sqrt(jnp.sum(kk * kk, axis=-1, keepdims=True)), KK_EPS)
    k = k * (1.0 + (a - 1.0) * k_a.astype(jnp.float32))
    r, k, v, decay, a = heads(r), heads(k), heads(v), heads(decay), heads(a)
    b = kk * a
    tm = lambda t: t.transpose(1, 0, 2, 3)
    S_final, y = lax.scan(_rwkv7_step, S0.astype(jnp.float32),
                          (tm(r), tm(decay), tm(k), tm(v), tm(kk), tm(b)))
    y = tm(y)
    mean = jnp.mean(y, axis=-1, keepdims=True)
    var = jnp.mean(jnp.square(y - mean), axis=-1, keepdims=True)
    y = ((y - mean) * lax.rsqrt(var + RW_GN_EPS) * ln_w.astype(jnp.float32).reshape(RW_HEADS, RW_HEAD_DIM)
         + ln_b.astype(jnp.float32).reshape(RW_HEADS, RW_HEAD_DIM))
    y = y + jnp.sum(r * k * r_k.astype(jnp.float32), axis=-1, keepdims=True) * v
    out = y.reshape(B, T, RW_WIDTH) * jax.nn.silu(gate)
    return out.astype(p.dtype), S_final, p[:, -1:]


def _trunk(h, hg_S, rw_S, shift, norm_pre, w_in, hg_lower_bounds, hg_norm, rw_mu, rw_w0, rw_w2,
           rw_a0, rw_a2, rw_k_k, rw_k_a, rw_r_k, rw_ln_w, rw_ln_b, w_out, norm_post):
    lbs = jnp.cumsum(jax.nn.softmax(hg_lower_bounds.astype(jnp.float32), axis=0), axis=0)
    new_hg, new_rw, new_shift = [], [], []
    for l in range(DEPTH):
        u = _rmsnorm(h, norm_pre[l])
        proj = u @ w_in[l]
        hg_out, hg_new = _hgrn2_mixer(proj[..., :HG_PROJ], lbs[l], hg_norm[l], hg_S[l])
        rw_out, rw_new, sh_new = _rwkv7_mixer(proj[..., HG_PROJ:], shift[l], rw_mu[l], rw_w0[l], rw_w2[l],
                                              rw_a0[l], rw_a2[l], rw_k_k[l], rw_k_a[l], rw_r_k[l],
                                              rw_ln_w[l], rw_ln_b[l], rw_S[l])
        mixed = jnp.concatenate([hg_out, rw_out], axis=-1)
        h = h + _rmsnorm(mixed @ w_out[l], norm_post[l])
        new_hg.append(hg_new)
        new_rw.append(rw_new)
        new_shift.append(sh_new)
    return h, jnp.stack(new_hg), jnp.stack(new_rw), jnp.stack(new_shift)


def setup_inputs(seed: int = 0) -> dict:
    key = jax.random.key(seed)
    ks = jax.random.split(key, 24)
    f32 = jnp.float32
    nrm = lambda k, shape, s: s * jax.random.normal(k, shape, f32)
    return {
        "x_prompt": nrm(ks[0], (BATCH, SEQ, D_MODEL), 1.0),
        "x_sample": nrm(ks[1], (DEC_BATCH, DEC_SEQ, D_MODEL), 1.0),
        "state_hgrn": nrm(ks[2], (DEPTH, DEC_BATCH, HG_HEADS, HG_HEAD_DIM, HG_HEAD_DIM), 0.5),
        "state_rwkv": nrm(ks[3], (DEPTH, DEC_BATCH, RW_HEADS, RW_HEAD_DIM, RW_HEAD_DIM), 1.0),
        "state_shift": nrm(ks[4], (DEPTH, DEC_BATCH, 1, RW_SHIFT_WIDTH), 1.0),
        "meta_tokens": nrm(ks[5], (N_META, D_MODEL), 1.0),
        "norm_pre": 1.0 + nrm(ks[6], (DEPTH, D_MODEL), 0.02),
        "w_in": nrm(ks[7], (DEPTH, D_MODEL, P_TOTAL), D_MODEL ** -0.5),
        "hg_lower_bounds": nrm(ks[8], (DEPTH + 1, HG_WIDTH), 1.0),
        "hg_norm": 1.0 + nrm(ks[9], (DEPTH, HG_WIDTH), 0.02),
        "rw_mu": jax.random.uniform(ks[10], (DEPTH, RW_SHIFT_WIDTH), f32),
        "rw_w0": jax.random.uniform(ks[11], (DEPTH, RW_WIDTH), f32, -6.0, -1.0),
        "rw_w2": nrm(ks[12], (DEPTH, RW_LORA, RW_WIDTH), 0.5 * RW_LORA ** -0.5),
        "rw_a0": nrm(ks[13], (DEPTH, RW_WIDTH), 0.5),
        "rw_a2": nrm(ks[14], (DEPTH, RW_LORA, RW_WIDTH), 0.5 * RW_LORA ** -0.5),
        "rw_k_k": 0.85 + nrm(ks[15], (DEPTH, RW_WIDTH), 0.02),
        "rw_k_a": 1.0 + nrm(ks[16], (DEPTH, RW_WIDTH), 0.02),
        "rw_r_k": nrm(ks[17], (DEPTH, RW_HEADS, RW_HEAD_DIM), 0.1),
        "rw_ln_w": 1.0 + nrm(ks[18], (DEPTH, RW_WIDTH), 0.02),
        "rw_ln_b": nrm(ks[19], (DEPTH, RW_WIDTH), 0.02),
        "w_out": nrm(ks[20], (DEPTH, MIX_WIDTH, D_MODEL), MIX_WIDTH ** -0.5),
        "norm_post": 1.0 + nrm(ks[21], (DEPTH, D_MODEL), 0.02),
    }


def reference(x_prompt, x_sample, state_hgrn, state_rwkv, state_shift, meta_tokens, norm_pre, w_in,
              hg_lower_bounds, hg_norm, rw_mu, rw_w0, rw_w2, rw_a0, rw_a2, rw_k_k, rw_k_a, rw_r_k,
              rw_ln_w, rw_ln_b, w_out, norm_post):
    weights = (norm_pre, w_in, hg_lower_bounds, hg_norm, rw_mu, rw_w0, rw_w2, rw_a0, rw_a2,
               rw_k_k, rw_k_a, rw_r_k, rw_ln_w, rw_ln_b, w_out, norm_post)
    B = x_prompt.shape[0]
    meta = jnp.broadcast_to(meta_tokens[None].astype(x_prompt.dtype), (B, N_META, D_MODEL))
    h0 = jnp.concatenate([meta, x_prompt], axis=1)
    hg0 = jnp.zeros((DEPTH, B, HG_HEADS, HG_HEAD_DIM, HG_HEAD_DIM), jnp.float32)
    rw0 = jnp.zeros((DEPTH, B, RW_HEADS, RW_HEAD_DIM, RW_HEAD_DIM), jnp.float32)
    sh0 = jnp.zeros((DEPTH, B, 1, RW_SHIFT_WIDTH), x_prompt.dtype)
    hp, hg_p, rw_p, sh_p = _trunk(h0, hg0, rw0, sh0, *weights)
    y_prompt = hp[:, N_META:]
    y_sample, hg_s, rw_s, sh_s = _trunk(x_sample, state_hgrn, state_rwkv, state_shift, *weights)
    return (y_prompt, y_sample, hg_p, rw_p, sh_p, hg_s, rw_s, sh_s)
```

```python
import functools

import jax
import jax.numpy as jnp
from jax import lax
from jax.experimental import pallas as pl
from jax.experimental.pallas import tpu as pltpu

F32 = jnp.float32
BF16 = jnp.bfloat16

LANES = 128
SUB = 8
D_MODEL = 2048
HG_HEADS = 8
HG_WIDTH = 1024
RW_PAIRS = 8
RW_HEAD = 64
RW_WIDTH = 1024
P_TOTAL = 8320
NBLK = P_TOTAL // LANES
HG_BLK = 32
RW_BLK = NBLK - HG_BLK
NORM_EPS = 1e-6
RW_GN_EPS = 64e-5
KK_EPS = 1e-12
VMEM_LIMIT = 56 * 1024 * 1024


def _bf(x):
    return x.astype(BF16)


def _dot(a, b):
    return jnp.dot(_bf(a), _bf(b), preferred_element_type=F32)


def _dot_nt(a, b):
    return lax.dot_general(_bf(a), _bf(b), (((1,), (1,)), ((), ())), preferred_element_type=F32)


def _dot_tn(a, b):
    return lax.dot_general(_bf(a), _bf(b), (((0,), (0,)), ((), ())), preferred_element_type=F32)


def _split3(x):
    hi = _bf(x)
    r = x - hi.astype(F32)
    mid = _bf(r)
    lo = _bf(r - mid.astype(F32))
    return hi, mid, lo


def _dot_exact_lhs(l_bf, parts):
    acc = jnp.dot(l_bf, parts[0], preferred_element_type=F32)
    acc += jnp.dot(l_bf, parts[1], preferred_element_type=F32)
    acc += jnp.dot(l_bf, parts[2], preferred_element_type=F32)
    return acc


def _group_sum(x, ones_bf):
    hi = _bf(x)
    lo = _bf(x - hi.astype(F32))
    return (jnp.dot(hi, ones_bf, preferred_element_type=F32)
            + jnp.dot(lo, ones_bf, preferred_element_type=F32))


def _sigmoid(x):
    return 1.0 / (1.0 + jnp.exp(-x))


def _inproj_kernel(x_ref, g_ref, w_ref, o_ref, u_ref, *, nb):
    @pl.when(pl.program_id(1) == 0)
    def _():
        x = x_ref[...]
        ms = jnp.mean(x * x, axis=-1, keepdims=True)
        u_ref[...] = _bf(x * lax.rsqrt(ms + NORM_EPS) * g_ref[...])

    acc = jnp.dot(u_ref[...], w_ref[...], preferred_element_type=F32)
    for cb in range(nb):
        o_ref[cb] = acc[:, cb * LANES:(cb + 1) * LANES]


def _inproj(x, g, w_bf, tm):
    rows = x.shape[0]
    nb = 13
    tn = nb * LANES
    return pl.pallas_call(
        functools.partial(_inproj_kernel, nb=nb),
        grid=(rows // tm, NBLK // nb),
        in_specs=[
            pl.BlockSpec((tm, D_MODEL), lambda i, j: (i, 0)),
            pl.BlockSpec((1, D_MODEL), lambda i, j: (0, 0)),
            pl.BlockSpec((D_MODEL, tn), lambda i, j: (0, j)),
        ],
        out_specs=pl.BlockSpec((nb, tm, LANES), lambda i, j: (j, i, 0)),
        out_shape=jax.ShapeDtypeStruct((NBLK, rows, LANES), F32),
        scratch_shapes=[pltpu.VMEM((tm, D_MODEL), BF16)],
        compiler_params=pltpu.CompilerParams(
            dimension_semantics=("arbitrary", "arbitrary"), vmem_limit_bytes=VMEM_LIMIT),
        name="inproj",
    )(x, g, w_bf)


def _mixer_kernel(proj_ref, hg0_ref, rw0_ref, sh0_ref, lb_ref, hgn_ref, mu_ref, w0_ref, a0_ref,
                  w2a_ref, kk_ref, ka_ref, rk_ref, lnw_ref, lnb_ref,
                  out_ref, hgS_ref, rwS_ref, sh_ref,
                  q_s, k_s, v_s, G_s, od_s, lora_s, *, C):
    c = pl.program_id(1)

    @pl.when(c == 0)
    def _():
        hgS_ref[...] = hg0_ref[...]
        rwS_ref[...] = rw0_ref[...]
        sh_ref[...] = sh0_ref[...]

    row = lax.broadcasted_iota(jnp.int32, (C, LANES), 0)
    lane = lax.broadcasted_iota(jnp.int32, (C, LANES), 1)
    ri = lax.broadcasted_iota(jnp.int32, (C, C), 0)
    ci = lax.broadcasted_iota(jnp.int32, (C, C), 1)
    tril_bf = (ci <= ri).astype(F32).astype(BF16)
    sub8 = lax.broadcasted_iota(jnp.int32, (SUB, LANES), 0)

    levels = [b for b in (8, 16, 32) if 2 * b <= C]
    ref_mats = [(ci <= (ri & (-2 * b)) + b - 1).astype(F32).astype(BF16) for b in levels]

    for h in range(HG_HEADS):
        pq = proj_ref[h]
        pf = proj_ref[HG_HEADS + h]
        v = proj_ref[2 * HG_HEADS + h]
        pg = proj_ref[3 * HG_HEADS + h]
        lb = lb_ref[:, h * LANES:(h + 1) * LANES]
        q = pq * _sigmoid(pq)
        forget = lb + (1.0 - lb) * _sigmoid(pf)
        g = jnp.log(forget)
        k = 1.0 - forget
        gparts = _split3(g)
        G = _dot_exact_lhs(tril_bf, gparts)
        St = hgS_ref[0, h]

        o = _dot_nt(q * jnp.exp(G), St)

        scores = jnp.zeros((C, C), F32)
        for b, m_bf in zip(levels, ref_mats):
            gref = _dot_exact_lhs(m_bf, gparts)
            right = (row & b) != 0
            d = G - gref
            e = jnp.exp(jnp.minimum(jnp.where(right, d, -d), 0.0))
            qb = jnp.where(right, q * e, 0.0)
            kb = jnp.where(right, 0.0, k * e)
            sc = _dot_nt(qb, kb)
            scores += jnp.where((ri & (-2 * b)) == (ci & (-2 * b)), sc, 0.0)
        if levels:
            o += _dot(scores, v)

        q_s[...] = q
        k_s[...] = k
        v_s[...] = v
        G_s[...] = G

        def sub_body(blk, carry):
            base = pl.multiple_of(blk * SUB, SUB)
            Gi = G_s[pl.ds(base, SUB), :]
            qi = q_s[pl.ds(base, SUB), :]
            acc = jnp.zeros((SUB, LANES), F32)
            for j in range(SUB):
                Gj = G_s[pl.ds(base + j, 1), :]
                kj = k_s[pl.ds(base + j, 1), :]
                vj = v_s[pl.ds(base + j, 1), :]
                e = jnp.exp(jnp.minimum(Gi - Gj, 0.0))
                p = jnp.where(sub8 >= j, qi * e * kj, 0.0)
                acc += jnp.sum(p, axis=-1, keepdims=True) * vj
            od_s[pl.ds(base, SUB), :] = acc
            return carry

        lax.fori_loop(0, C // SUB, sub_body, 0)
        o += od_s[...]

        Gl = G[C - 1:C, :]
        hgS_ref[0, h] = St * jnp.exp(Gl) + _dot_tn(v, k * jnp.exp(Gl - G))

        o = o * lax.rsqrt(jnp.mean(o * o, axis=-1, keepdims=True) + NORM_EPS)
        o = o * hgn_ref[:, h * LANES:(h + 1) * LANES] * (pg * _sigmoid(pg))
        out_ref[:, h * LANES:(h + 1) * LANES] = o.astype(out_ref.dtype)

    def shifted(blk):
        p = proj_ref[HG_BLK + blk]
        prev = jnp.where(row == 0, sh_ref[0, blk], pltpu.roll(p, 1, axis=0))
        return p + (prev - p) * mu_ref[blk]

    xl = shifted(RW_BLK - 1)
    xl = jnp.where(lane < RW_HEAD, jnp.tanh(xl), xl)
    lora_s[...] = jnp.dot(_bf(xl), w2a_ref[...], preferred_element_type=F32)

    head0 = lane < RW_HEAD
    blockdiag = ((lax.broadcasted_iota(jnp.int32, (LANES, LANES), 0) < RW_HEAD)
                 == (lax.broadcasted_iota(jnp.int32, (LANES, LANES), 1) < RW_HEAD))
    ones_bf = blockdiag.astype(F32).astype(BF16)
    C2 = 2 * C
    ri2 = lax.broadcasted_iota(jnp.int32, (C, C2), 0)
    ci2 = lax.broadcasted_iota(jnp.int32, (C, C2), 1)
    cj2 = jnp.where(ci2 >= C, ci2 - C, ci2)
    smask = cj2 < ri2
    imask = cj2 <= ri2
    first = ci2 < C
    eye2 = (lax.broadcasted_iota(jnp.int32, (C2, C2), 0)
            == lax.broadcasted_iota(jnp.int32, (C2, C2), 1)).astype(F32)
    vl0 = lax.broadcasted_iota(jnp.int32, (C, LANES), 1) < RW_HEAD
    n_dbl = C.bit_length() - 1

    for p in range(RW_PAIRS):
        sl = slice(p * LANES, (p + 1) * LANES)
        r = shifted(p)
        xk = shifted(RW_PAIRS + p)
        v = shifted(2 * RW_PAIRS + p)
        xg = shifted(3 * RW_PAIRS + p)
        w = w0_ref[:, sl] + lora_s[:, sl]
        lw = -0.6065306597126334 * _sigmoid(w)
        a = _sigmoid(a0_ref[:, sl] + lora_s[:, RW_WIDTH + p * LANES:RW_WIDTH + (p + 1) * LANES])
        kkr = xk * kk_ref[:, sl]
        ss = _group_sum(kkr * kkr, ones_bf)
        kk = kkr / jnp.maximum(jnp.sqrt(ss), KK_EPS)
        k = xk * (1.0 + (a - 1.0) * ka_ref[:, sl])
        b = kk * a

        gam = _dot_exact_lhs(tril_bf, _split3(lw))
        eg = jnp.exp(gam)
        eng = jnp.exp(-gam)
        at = -kk * jnp.exp(gam - lw)
        bt = b * eng
        kt = k * eng
        rt = r * eg

        l0 = jnp.concatenate([jnp.where(head0, at, 0.0), jnp.where(head0, rt, 0.0)], axis=0)
        l1 = jnp.concatenate([jnp.where(head0, 0.0, at), jnp.where(head0, 0.0, rt)], axis=0)
        m0 = _dot_nt(l0, jnp.concatenate([bt, kt], axis=0))
        m1 = _dot_nt(l1, jnp.concatenate([kt, bt], axis=0))
        aa0 = jnp.where(smask, m0[:C], 0.0)
        ar0 = jnp.where(imask, m0[C:], 0.0)
        aa1 = jnp.where(smask, m1[:C], 0.0)
        ar1 = jnp.where(imask, m1[C:], 0.0)

        an = jnp.concatenate([jnp.where(first, aa0, 0.0), jnp.where(first, 0.0, aa1)], axis=0)
        t = eye2 + an
        for _ in range(n_dbl - 1):
            an = _dot(an, an)
            t = t + _dot(an, t)

        S = rwS_ref[0, p]
        wst = _dot_nt(jnp.concatenate([at, rt], axis=0), S)
        vv = jnp.concatenate([v, v], axis=0)
        akv = jnp.where(vl0,
                        _dot(jnp.where(first, 0.0, aa0), vv),
                        _dot(jnp.where(first, aa1, 0.0), vv))
        rhs = wst[:C] + akv
        tu = _dot(t, jnp.concatenate([jnp.where(vl0, rhs, 0.0), jnp.where(vl0, 0.0, rhs)], axis=0))
        u = tu[:C] + tu[C:]
        uv = jnp.concatenate([u, v], axis=0)
        vu = jnp.concatenate([v, u], axis=0)
        y = wst[C:] + jnp.where(vl0, _dot(ar0, uv), _dot(ar1, vu))

        dl = jnp.exp(gam[C - 1:C, :] - gam)
        upd = _dot_tn(uv, jnp.concatenate([b * dl, k * dl], axis=0))
        rwS_ref[0, p] = S * eg[C - 1:C, :] + jnp.where(blockdiag, upd, 0.0)

        mean = _group_sum(y, ones_bf) * (1.0 / RW_HEAD)
        yc = y - mean
        var = _group_sum(yc * yc, ones_bf) * (1.0 / RW_HEAD)
        yn = yc * lax.rsqrt(var + RW_GN_EPS) * lnw_ref[:, sl] + lnb_ref[:, sl]
        bonus = _group_sum(r * k * rk_ref[:, sl], ones_bf)
        yo = (yn + bonus * v) * (xg * _sigmoid(xg))
        out_ref[:, HG_WIDTH + p * LANES:HG_WIDTH + (p + 1) * LANES] = yo.astype(out_ref.dtype)

    for blk in range(RW_BLK):
        sh_ref[0, blk] = proj_ref[HG_BLK + blk, C - 1:C, :]


def _mixer(proj, hg0, rw0, sh0, params, B, T, C):
    nc = T // C
    full = lambda shape: pl.BlockSpec(shape, lambda b, c: (0,) * len(shape))
    per_b = lambda shape: pl.BlockSpec((1,) + shape, lambda b, c: (b,) + (0,) * len(shape))
    lb, hgn, mu, w0, a0, w2a, kk, ka, rk, lnw, lnb = params
    in_specs = [
        pl.BlockSpec((NBLK, C, LANES), lambda b, c: (0, b * nc + c, 0)),
        per_b((HG_HEADS, LANES, LANES)),
        per_b((RW_PAIRS, LANES, LANES)),
        per_b((RW_BLK, 1, LANES)),
        full((1, HG_WIDTH)), full((1, HG_WIDTH)), full((RW_BLK, 1, LANES)),
        full((1, RW_WIDTH)), full((1, RW_WIDTH)), full((LANES, 2 * RW_WIDTH)),
        full((1, RW_WIDTH)), full((1, RW_WIDTH)), full((1, RW_WIDTH)),
        full((1, RW_WIDTH)), full((1, RW_WIDTH)),
    ]
    out_specs = [
        pl.BlockSpec((C, D_MODEL), lambda b, c: (b * nc + c, 0)),
        per_b((HG_HEADS, LANES, LANES)),
        per_b((RW_PAIRS, LANES, LANES)),
        per_b((RW_BLK, 1, LANES)),
    ]
    out_shape = [
        jax.ShapeDtypeStruct((B * T, D_MODEL), BF16),
        jax.ShapeDtypeStruct((B, HG_HEADS, LANES, LANES), F32),
        jax.ShapeDtypeStruct((B, RW_PAIRS, LANES, LANES), F32),
        jax.ShapeDtypeStruct((B, RW_BLK, 1, LANES), F32),
    ]
    scratch = [pltpu.VMEM((C, LANES), F32)] * 5 + [pltpu.VMEM((C, 2 * RW_WIDTH), F32)]
    return pl.pallas_call(
        functools.partial(_mixer_kernel, C=C),
        grid=(B, nc),
        in_specs=in_specs,
        out_specs=out_specs,
        out_shape=out_shape,
        scratch_shapes=scratch,
        compiler_params=pltpu.CompilerParams(
            dimension_semantics=("arbitrary", "arbitrary"), vmem_limit_bytes=VMEM_LIMIT),
        name="mixer",
    )(proj, hg0, rw0, sh0, lb, hgn, mu, w0, a0, w2a, kk, ka, rk, lnw, lnb)


def _outproj_kernel(m_ref, w_ref, g_ref, x_ref, o_ref):
    z = jnp.dot(m_ref[...], w_ref[...], preferred_element_type=F32)
    ms = jnp.mean(z * z, axis=-1, keepdims=True)
    o_ref[...] = x_ref[...] + z * lax.rsqrt(ms + NORM_EPS) * g_ref[...]


def _outproj(mixed, w_bf, g, x, tm):
    rows = x.shape[0]
    return pl.pallas_call(
        _outproj_kernel,
        grid=(rows // tm,),
        in_specs=[
            pl.BlockSpec((tm, D_MODEL), lambda i: (i, 0)),
            pl.BlockSpec((D_MODEL, D_MODEL), lambda i: (0, 0)),
            pl.BlockSpec((1, D_MODEL), lambda i: (0, 0)),
            pl.BlockSpec((tm, D_MODEL), lambda i: (i, 0)),
        ],
        out_specs=pl.BlockSpec((tm, D_MODEL), lambda i: (i, 0)),
        out_shape=jax.ShapeDtypeStruct((rows, D_MODEL), F32),
        compiler_params=pltpu.CompilerParams(
            dimension_semantics=("arbitrary",), vmem_limit_bytes=VMEM_LIMIT),
        name="outproj",
    )(mixed, w_bf, g, x)


def _rw_pack(s):
    B = s.shape[0]
    s = s.reshape(B, RW_PAIRS, 2, RW_HEAD, RW_HEAD)
    z = jnp.zeros_like(s[:, :, 0])
    top = jnp.concatenate([s[:, :, 0], z], axis=-1)
    bot = jnp.concatenate([z, s[:, :, 1]], axis=-1)
    return jnp.concatenate([top, bot], axis=-2)


def _rw_unpack(s):
    B = s.shape[0]
    a = s[:, :, :RW_HEAD, :RW_HEAD]
    b = s[:, :, RW_HEAD:, RW_HEAD:]
    return jnp.stack([a, b], axis=2).reshape(B, 2 * RW_PAIRS, RW_HEAD, RW_HEAD)


def _pick_tile(rows, cap):
    t = cap
    while rows % t:
        t //= 2
    return t


def _trunk(x, hg0, rw0, sh0, weights, C, need_y=True):
    (norm_pre, w_in_bf, mixer_params, w_out_bf, norm_post) = weights
    B, T, _ = x.shape
    x2 = x.reshape(B * T, D_MODEL)
    proj = _inproj(x2, norm_pre, w_in_bf, _pick_tile(B * T, 512))
    mixed, hg, rw, sh = _mixer(proj, hg0, rw0, sh0, mixer_params, B, T, C)
    y = None
    if need_y:
        y = _outproj(mixed, w_out_bf, norm_post, x2, _pick_tile(B * T, 512)).reshape(B, T, D_MODEL)
    return y, hg, rw, sh


def kernel(x_prompt, x_sample, state_hgrn, state_rwkv, state_shift, meta_tokens, norm_pre, w_in,
           hg_lower_bounds, hg_norm, rw_mu, rw_w0, rw_w2, rw_a0, rw_a2, rw_k_k, rw_k_a, rw_r_k,
           rw_ln_w, rw_ln_b, w_out, norm_post):
    B = x_prompt.shape[0]
    Bs = x_sample.shape[0]
    n_meta = meta_tokens.shape[0]
    l = 0

    lbs = jnp.cumsum(jax.nn.softmax(hg_lower_bounds.astype(F32), axis=0), axis=0)[l].reshape(1, HG_WIDTH)
    z64 = jnp.zeros((RW_HEAD, RW_WIDTH), F32)
    w2a = jnp.concatenate([jnp.concatenate([rw_w2[l], z64], axis=1),
                           jnp.concatenate([z64, rw_a2[l]], axis=1)], axis=0).astype(BF16)
    row = lambda t: t.reshape(1, -1).astype(F32)
    mixer_params = (lbs, row(hg_norm[l]), rw_mu[l].reshape(RW_BLK, 1, LANES), row(rw_w0[l]), row(rw_a0[l]),
                    w2a, row(rw_k_k[l]), row(rw_k_a[l]), row(rw_r_k[l]), row(rw_ln_w[l]), row(rw_ln_b[l]))
    weights = (row(norm_pre[l]), w_in[l].astype(BF16), mixer_params, w_out[l].astype(BF16), row(norm_post[l]))

    zero_hg = jnp.zeros((1, HG_HEADS, LANES, LANES), F32)
    zero_rw = jnp.zeros((1, RW_PAIRS, LANES, LANES), F32)
    zero_sh = jnp.zeros((1, RW_BLK, 1, LANES), F32)
    _, hg_m, rw_m, sh_m = _trunk(meta_tokens[None].astype(F32), zero_hg, zero_rw, zero_sh, weights,
                                 C=n_meta, need_y=False)

    rep = lambda t: jnp.broadcast_to(t, (B,) + t.shape[1:])
    y_p, hg_p, rw_p, sh_p = _trunk(x_prompt, rep(hg_m), rep(rw_m), rep(sh_m), weights, C=64)

    hg_s0 = jnp.swapaxes(state_hgrn[l], -1, -2)
    rw_s0 = _rw_pack(state_rwkv[l])
    sh_s0 = state_shift[l].reshape(Bs, RW_BLK, 1, LANES)
    y_s, hg_s, rw_s, sh_s = _trunk(x_sample, hg_s0, rw_s0, sh_s0, weights, C=64)

    unhg = lambda t: jnp.swapaxes(t, -1, -2)[None]
    unsh = lambda t: t.reshape(t.shape[0], 1, RW_BLK * LANES)[None]
    return (y_p, y_s,
            unhg(hg_p), _rw_unpack(rw_p)[None], unsh(sh_p),
            unhg(hg_s), _rw_unpack(rw_s)[None], unsh(sh_s))
```

```python
import functools

import jax
import jax.numpy as jnp
from jax import lax
from jax.experimental import pallas as pl
from jax.experimental.pallas import tpu as pltpu

F32 = jnp.float32
BF16 = jnp.bfloat16

LANES = 128
SUB = 8
D_MODEL = 2048
HG_HEADS = 8
HG_WIDTH = 1024
RW_PAIRS = 8
RW_HEAD = 64
RW_WIDTH = 1024
P_TOTAL = 8320
NBLK = P_TOTAL // LANES
HG_BLK = 32
RW_BLK = NBLK - HG_BLK
NORM_EPS = 1e-6
RW_GN_EPS = 64e-5
KK_EPS = 1e-12
VMEM_LIMIT = 56 * 1024 * 1024


def _bf(x):
    return x.astype(BF16)


def _dot(a, b):
    return jnp.dot(_bf(a), _bf(b), preferred_element_type=F32)


def _dot_nt(a, b):
    return lax.dot_general(_bf(a), _bf(b), (((1,), (1,)), ((), ())), preferred_element_type=F32)


def _dot_tn(a, b):
    return lax.dot_general(_bf(a), _bf(b), (((0,), (0,)), ((), ())), preferred_element_type=F32)


def _split3(x):
    hi = _bf(x)
    r = x - hi.astype(F32)
    mid = _bf(r)
    lo = _bf(r - mid.astype(F32))
    return hi, mid, lo


def _dot_exact_lhs(l_bf, parts):
    acc = jnp.dot(l_bf, parts[0], preferred_element_type=F32)
    acc += jnp.dot(l_bf, parts[1], preferred_element_type=F32)
    acc += jnp.dot(l_bf, parts[2], preferred_element_type=F32)
    return acc


def _group_sum(x, ones_bf):
    hi = _bf(x)
    lo = _bf(x - hi.astype(F32))
    return (jnp.dot(hi, ones_bf, preferred_element_type=F32)
            + jnp.dot(lo, ones_bf, preferred_element_type=F32))


def _sigmoid(x):
    return 1.0 / (1.0 + jnp.exp(-x))


def _inproj_kernel(x_ref, g_ref, w_ref, o_ref, u_ref, *, nb):
    @pl.when(pl.program_id(1) == 0)
    def _():
        x = x_ref[...]
        ms = jnp.mean(x * x, axis=-1, keepdims=True)
        u_ref[...] = _bf(x * lax.rsqrt(ms + NORM_EPS) * g_ref[...])

    acc = jnp.dot(u_ref[...], w_ref[...], preferred_element_type=F32)
    for cb in range(nb):
        o_ref[cb] = acc[:, cb * LANES:(cb + 1) * LANES]


def _inproj(x, g, w_bf, tm):
    rows = x.shape[0]
    nb = 13
    tn = nb * LANES
    return pl.pallas_call(
        functools.partial(_inproj_kernel, nb=nb),
        grid=(rows // tm, NBLK // nb),
        in_specs=[
            pl.BlockSpec((tm, D_MODEL), lambda i, j: (i, 0)),
            pl.BlockSpec((1, D_MODEL), lambda i, j: (0, 0)),
            pl.BlockSpec((D_MODEL, tn), lambda i, j: (0, j)),
        ],
        out_specs=pl.BlockSpec((nb, tm, LANES), lambda i, j: (j, i, 0)),
        out_shape=jax.ShapeDtypeStruct((NBLK, rows, LANES), F32),
        scratch_shapes=[pltpu.VMEM((tm, D_MODEL), BF16)],
        compiler_params=pltpu.CompilerParams(
            dimension_semantics=("arbitrary", "arbitrary"), vmem_limit_bytes=VMEM_LIMIT),
        name="inproj",
    )(x, g, w_bf)


def _mixer_kernel(proj_ref, hg0_ref, rw0_ref, sh0_ref, lb_ref, hgn_ref, mu_ref, w0_ref, a0_ref,
                  w2a_ref, kk_ref, ka_ref, rk_ref, lnw_ref, lnb_ref, wsel_ref,
                  out_ref, hgS_ref, rwS_ref, sh_ref,
                  q_s, k_s, g_s, lw_s, gam_s, cum_s, lora_s, *, C):
    c = pl.program_id(1)

    @pl.when(c == 0)
    def _():
        hgS_ref[...] = hg0_ref[...]
        rwS_ref[...] = rw0_ref[...]
        sh_ref[...] = sh0_ref[...]

    row = lax.broadcasted_iota(jnp.int32, (C, LANES), 0)
    lane = lax.broadcasted_iota(jnp.int32, (C, LANES), 1)
    ri = lax.broadcasted_iota(jnp.int32, (C, C), 0)
    ci = lax.broadcasted_iota(jnp.int32, (C, C), 1)
    tril_bf = (ci <= ri).astype(F32).astype(BF16)

    levels = [b for b in (8, 16, 32) if 2 * b <= C]
    cum_mats = [tril_bf] + [(ci <= (ri & (-2 * b)) + b - 1).astype(F32).astype(BF16) for b in levels]

    for h in range(HG_HEADS):
        hs = slice(h * LANES, (h + 1) * LANES)
        pq = proj_ref[h]
        lb = lb_ref[:, hs]
        forget = lb + (1.0 - lb) * _sigmoid(proj_ref[HG_HEADS + h])
        q_s[:, hs] = pq * _sigmoid(pq)
        k_s[:, hs] = 1.0 - forget
        g_s[:, hs] = jnp.log(forget)
    cum_s[...] = _dot_exact_lhs(jnp.concatenate(cum_mats, axis=0), _split3(g_s[...]))

    def two_rows(ref, base, hs):
        return jnp.concatenate([jnp.broadcast_to(ref[base:base + 1, hs], (SUB, LANES)),
                                jnp.broadcast_to(ref[base + SUB:base + SUB + 1, hs], (SUB, LANES))], axis=0)

    scores_off = []
    diag_lhs = []
    for h in range(HG_HEADS):
        hs = slice(h * LANES, (h + 1) * LANES)
        q = q_s[:, hs]
        k = k_s[:, hs]
        G = cum_s[0:C, hs]
        scores = jnp.zeros((C, C), F32)
        for n, b in enumerate(levels):
            right = (row & b) != 0
            d = G - cum_s[(n + 1) * C:(n + 2) * C, hs]
            e = jnp.exp(jnp.minimum(jnp.where(right, d, -d), 0.0))
            sc = _dot_nt(jnp.where(right, q * e, 0.0), jnp.where(right, 0.0, k * e))
            scores += jnp.where((ri & (-2 * b)) == (ci & (-2 * b)), sc, 0.0)
        scores_off.append(scores)

        units = []
        for u in range(C // (2 * SUB)):
            r0 = u * 2 * SUB
            G16 = G[r0:r0 + 2 * SUB]
            q16 = q[r0:r0 + 2 * SUB]
            cols = []
            for j in range(SUB):
                e = jnp.exp(jnp.minimum(G16 - two_rows(cum_s, r0 + j, hs), 0.0))
                cols.append(_bf(q16 * e * two_rows(k_s, r0 + j, hs)))
            units.append(jnp.concatenate(cols, axis=1))
        diag_lhs.append(jnp.concatenate(units, axis=0))
    diag = jnp.dot(jnp.concatenate(diag_lhs, axis=0), wsel_ref[...], preferred_element_type=F32)
    diag_keep = ((lane >> 3) == (row >> 3)) & (lane <= row)

    for h in range(HG_HEADS):
        hs = slice(h * LANES, (h + 1) * LANES)
        q = q_s[:, hs]
        k = k_s[:, hs]
        G = cum_s[0:C, hs]
        v = proj_ref[2 * HG_HEADS + h]
        pg = proj_ref[3 * HG_HEADS + h]
        St = hgS_ref[0, h]
        scores = scores_off[h] + jnp.where(diag_keep, diag[h * C:(h + 1) * C], 0.0)[:, :C]
        o = _dot_nt(q * jnp.exp(G), St) + _dot(scores, v)

        Gl = G[C - 1:C, :]
        hgS_ref[0, h] = St * jnp.exp(Gl) + _dot_tn(v, k * jnp.exp(Gl - G))

        o = o * lax.rsqrt(jnp.mean(o * o, axis=-1, keepdims=True) + NORM_EPS)
        o = o * hgn_ref[:, hs] * (pg * _sigmoid(pg))
        out_ref[:, hs] = o.astype(out_ref.dtype)

    def shifted(blk):
        p = proj_ref[HG_BLK + blk]
        prev = jnp.where(row == 0, sh_ref[0, blk], pltpu.roll(p, 1, axis=0))
        return p + (prev - p) * mu_ref[blk]

    xl = shifted(RW_BLK - 1)
    xl = jnp.where(lane < RW_HEAD, jnp.tanh(xl), xl)
    lora_s[...] = jnp.dot(_bf(xl), w2a_ref[...], preferred_element_type=F32)

    head0 = lane < RW_HEAD
    blockdiag = ((lax.broadcasted_iota(jnp.int32, (LANES, LANES), 0) < RW_HEAD)
                 == (lax.broadcasted_iota(jnp.int32, (LANES, LANES), 1) < RW_HEAD))
    ones_bf = blockdiag.astype(F32).astype(BF16)
    C2 = 2 * C
    ri2 = lax.broadcasted_iota(jnp.int32, (C, C2), 0)
    ci2 = lax.broadcasted_iota(jnp.int32, (C, C2), 1)
    cj2 = jnp.where(ci2 >= C, ci2 - C, ci2)
    smask = cj2 < ri2
    imask = cj2 <= ri2
    first = ci2 < C
    eye2 = (lax.broadcasted_iota(jnp.int32, (C2, C2), 0)
            == lax.broadcasted_iota(jnp.int32, (C2, C2), 1)).astype(F32)
    vl0 = lax.broadcasted_iota(jnp.int32, (C, LANES), 1) < RW_HEAD
    n_dbl = C.bit_length() - 1

    pairs = range(RW_PAIRS)
    sls = [slice(p * LANES, (p + 1) * LANES) for p in pairs]
    st = [dict() for _ in pairs]

    for p, d in zip(pairs, st):
        sl = sls[p]
        d["r"] = shifted(p)
        xk = shifted(RW_PAIRS + p)
        d["v"] = shifted(2 * RW_PAIRS + p)
        w = w0_ref[:, sl] + lora_s[:, sl]
        lw_s[:, sl] = -0.6065306597126334 * _sigmoid(w)
        a = _sigmoid(a0_ref[:, sl] + lora_s[:, RW_WIDTH + p * LANES:RW_WIDTH + (p + 1) * LANES])
        kkr = xk * kk_ref[:, sl]
        kk = kkr / jnp.maximum(jnp.sqrt(_group_sum(kkr * kkr, ones_bf)), KK_EPS)
        d["kk"] = kk
        d["k"] = xk * (1.0 + (a - 1.0) * ka_ref[:, sl])
        d["b"] = kk * a
        d["bonus"] = _group_sum(d["r"] * d["k"] * rk_ref[:, sl], ones_bf)

    gam_s[...] = _dot_exact_lhs(tril_bf, _split3(lw_s[...]))

    for p, d in zip(pairs, st):
        sl = sls[p]
        gam = gam_s[:, sl]
        eg = jnp.exp(gam)
        eng = jnp.exp(-gam)
        at = -d["kk"] * jnp.exp(gam - lw_s[:, sl])
        rt = d["r"] * eg
        bt = d["b"] * eng
        kt = d["k"] * eng
        dl = eg[C - 1:C, :] * eng
        d["bk_dl"] = jnp.concatenate([d["b"] * dl, d["k"] * dl], axis=0)
        d["eg_last"] = eg[C - 1:C, :]
        d["at_rt"] = jnp.concatenate([at, rt], axis=0)
        l0 = jnp.concatenate([jnp.where(head0, at, 0.0), jnp.where(head0, rt, 0.0)], axis=0)
        l1 = jnp.concatenate([jnp.where(head0, 0.0, at), jnp.where(head0, 0.0, rt)], axis=0)
        m0 = _dot_nt(l0, jnp.concatenate([bt, kt], axis=0))
        m1 = _dot_nt(l1, jnp.concatenate([kt, bt], axis=0))
        d["aa0"] = jnp.where(smask, m0[:C], 0.0)
        d["ar0"] = jnp.where(imask, m0[C:], 0.0)
        d["aa1"] = jnp.where(smask, m1[:C], 0.0)
        d["ar1"] = jnp.where(imask, m1[C:], 0.0)
        d["an"] = jnp.concatenate([jnp.where(first, d["aa0"], 0.0), jnp.where(first, 0.0, d["aa1"])], axis=0)
        d["t"] = eye2 + d["an"]

    for _ in range(n_dbl - 1):
        for d in st:
            d["an"] = _dot(d["an"], d["an"])
        for d in st:
            d["t"] = d["t"] + _dot(d["an"], d["t"])

    for p, d in zip(pairs, st):
        d["S"] = rwS_ref[0, p]
        d["wst"] = _dot_nt(d["at_rt"], d["S"])
        vv = jnp.concatenate([d["v"], d["v"]], axis=0)
        akv = jnp.where(vl0,
                        _dot(jnp.where(first, 0.0, d["aa0"]), vv),
                        _dot(jnp.where(first, d["aa1"], 0.0), vv))
        rhs = d["wst"][:C] + akv
        d["rhs2"] = jnp.concatenate([jnp.where(vl0, rhs, 0.0), jnp.where(vl0, 0.0, rhs)], axis=0)

    for d in st:
        tu = _dot(d["t"], d["rhs2"])
        d["u"] = tu[:C] + tu[C:]

    for p, d in zip(pairs, st):
        u, v = d["u"], d["v"]
        uv = jnp.concatenate([u, v], axis=0)
        vu = jnp.concatenate([v, u], axis=0)
        d["y"] = d["wst"][C:] + jnp.where(vl0, _dot(d["ar0"], uv), _dot(d["ar1"], vu))
        upd = _dot_tn(uv, d["bk_dl"])
        rwS_ref[0, p] = d["S"] * d["eg_last"] + jnp.where(blockdiag, upd, 0.0)

    for d in st:
        d["yc"] = d["y"] - _group_sum(d["y"], ones_bf) * (1.0 / RW_HEAD)
    for p, d in zip(pairs, st):
        sl = sls[p]
        var = _group_sum(d["yc"] * d["yc"], ones_bf) * (1.0 / RW_HEAD)
        yn = d["yc"] * lax.rsqrt(var + RW_GN_EPS) * lnw_ref[:, sl] + lnb_ref[:, sl]
        xg = shifted(3 * RW_PAIRS + p)
        yo = (yn + d["bonus"] * d["v"]) * (xg * _sigmoid(xg))
        out_ref[:, HG_WIDTH + p * LANES:HG_WIDTH + (p + 1) * LANES] = yo.astype(out_ref.dtype)

    for blk in range(RW_BLK):
        sh_ref[0, blk] = proj_ref[HG_BLK + blk, C - 1:C, :]


def _mixer(proj, hg0, rw0, sh0, params, B, T, C):
    nc = T // C
    full = lambda shape: pl.BlockSpec(shape, lambda b, c: (0,) * len(shape))
    per_b = lambda shape: pl.BlockSpec((1,) + shape, lambda b, c: (b,) + (0,) * len(shape))
    lb, hgn, mu, w0, a0, w2a, kk, ka, rk, lnw, lnb = params
    wsel = ((jnp.arange(SUB * LANES)[:, None] // LANES) == (jnp.arange(LANES)[None, :] % SUB)).astype(BF16)
    n_cum = 1 + len([b for b in (8, 16, 32) if 2 * b <= C])
    in_specs = [
        pl.BlockSpec((NBLK, C, LANES), lambda b, c: (0, b * nc + c, 0)),
        per_b((HG_HEADS, LANES, LANES)),
        per_b((RW_PAIRS, LANES, LANES)),
        per_b((RW_BLK, 1, LANES)),
        full((1, HG_WIDTH)), full((1, HG_WIDTH)), full((RW_BLK, 1, LANES)),
        full((1, RW_WIDTH)), full((1, RW_WIDTH)), full((LANES, 2 * RW_WIDTH)),
        full((1, RW_WIDTH)), full((1, RW_WIDTH)), full((1, RW_WIDTH)),
        full((1, RW_WIDTH)), full((1, RW_WIDTH)), full((SUB * LANES, LANES)),
    ]
    out_specs = [
        pl.BlockSpec((C, D_MODEL), lambda b, c: (b * nc + c, 0)),
        per_b((HG_HEADS, LANES, LANES)),
        per_b((RW_PAIRS, LANES, LANES)),
        per_b((RW_BLK, 1, LANES)),
    ]
    out_shape = [
        jax.ShapeDtypeStruct((B * T, D_MODEL), BF16),
        jax.ShapeDtypeStruct((B, HG_HEADS, LANES, LANES), F32),
        jax.ShapeDtypeStruct((B, RW_PAIRS, LANES, LANES), F32),
        jax.ShapeDtypeStruct((B, RW_BLK, 1, LANES), F32),
    ]
    scratch = ([pltpu.VMEM((C, HG_WIDTH), F32)] * 5 + [pltpu.VMEM((n_cum * C, HG_WIDTH), F32)]
               + [pltpu.VMEM((C, 2 * RW_WIDTH), F32)])
    return pl.pallas_call(
        functools.partial(_mixer_kernel, C=C),
        grid=(B, nc),
        in_specs=in_specs,
        out_specs=out_specs,
        out_shape=out_shape,
        scratch_shapes=scratch,
        compiler_params=pltpu.CompilerParams(
            dimension_semantics=("arbitrary", "arbitrary"), vmem_limit_bytes=VMEM_LIMIT),
        name="mixer",
    )(proj, hg0, rw0, sh0, lb, hgn, mu, w0, a0, w2a, kk, ka, rk, lnw, lnb, wsel)


def _outproj_kernel(m_ref, w_ref, g_ref, x_ref, o_ref):
    z = jnp.dot(m_ref[...], w_ref[...], preferred_element_type=F32)
    ms = jnp.mean(z * z, axis=-1, keepdims=True)
    o_ref[...] = x_ref[...] + z * lax.rsqrt(ms + NORM_EPS) * g_ref[...]


def _outproj(mixed, w_bf, g, x, tm):
    rows = x.shape[0]
    return pl.pallas_call(
        _outproj_kernel,
        grid=(rows // tm,),
        in_specs=[
            pl.BlockSpec((tm, D_MODEL), lambda i: (i, 0)),
            pl.BlockSpec((D_MODEL, D_MODEL), lambda i: (0, 0)),
            pl.BlockSpec((1, D_MODEL), lambda i: (0, 0)),
            pl.BlockSpec((tm, D_MODEL), lambda i: (i, 0)),
        ],
        out_specs=pl.BlockSpec((tm, D_MODEL), lambda i: (i, 0)),
        out_shape=jax.ShapeDtypeStruct((rows, D_MODEL), F32),
        compiler_params=pltpu.CompilerParams(
            dimension_semantics=("arbitrary",), vmem_limit_bytes=VMEM_LIMIT),
        name="outproj",
    )(mixed, w_bf, g, x)


def _rw_pack(s):
    B = s.shape[0]
    s = s.reshape(B, RW_PAIRS, 2, RW_HEAD, RW_HEAD)
    z = jnp.zeros_like(s[:, :, 0])
    top = jnp.concatenate([s[:, :, 0], z], axis=-1)
    bot = jnp.concatenate([z, s[:, :, 1]], axis=-1)
    return jnp.concatenate([top, bot], axis=-2)


def _rw_unpack(s):
    B = s.shape[0]
    a = s[:, :, :RW_HEAD, :RW_HEAD]
    b = s[:, :, RW_HEAD:, RW_HEAD:]
    return jnp.stack([a, b], axis=2).reshape(B, 2 * RW_PAIRS, RW_HEAD, RW_HEAD)


def _pick_tile(rows, cap):
    t = cap
    while rows % t:
        t //= 2
    return t


def _trunk(x, hg0, rw0, sh0, weights, C, need_y=True):
    (norm_pre, w_in_bf, mixer_params, w_out_bf, norm_post) = weights
    B, T, _ = x.shape
    x2 = x.reshape(B * T, D_MODEL)
    proj = _inproj(x2, norm_pre, w_in_bf, _pick_tile(B * T, 512))
    mixed, hg, rw, sh = _mixer(proj, hg0, rw0, sh0, mixer_params, B, T, C)
    y = None
    if need_y:
        y = _outproj(mixed, w_out_bf, norm_post, x2, _pick_tile(B * T, 512)).reshape(B, T, D_MODEL)
    return y, hg, rw, sh


def kernel(x_prompt, x_sample, state_hgrn, state_rwkv, state_shift, meta_tokens, norm_pre, w_in,
           hg_lower_bounds, hg_norm, rw_mu, rw_w0, rw_w2, rw_a0, rw_a2, rw_k_k, rw_k_a, rw_r_k,
           rw_ln_w, rw_ln_b, w_out, norm_post):
    B = x_prompt.shape[0]
    Bs = x_sample.shape[0]
    n_meta = meta_tokens.shape[0]
    l = 0

    lbs = jnp.cumsum(jax.nn.softmax(hg_lower_bounds.astype(F32), axis=0), axis=0)[l].reshape(1, HG_WIDTH)
    z64 = jnp.zeros((RW_HEAD, RW_WIDTH), F32)
    w2a = jnp.concatenate([jnp.concatenate([rw_w2[l], z64], axis=1),
                           jnp.concatenate([z64, rw_a2[l]], axis=1)], axis=0).astype(BF16)
    row = lambda t: t.reshape(1, -1).astype(F32)
    mixer_params = (lbs, row(hg_norm[l]), rw_mu[l].reshape(RW_BLK, 1, LANES), row(rw_w0[l]), row(rw_a0[l]),
                    w2a, row(rw_k_k[l]), row(rw_k_a[l]), row(rw_r_k[l]), row(rw_ln_w[l]), row(rw_ln_b[l]))
    weights = (row(norm_pre[l]), w_in[l].astype(BF16), mixer_params, w_out[l].astype(BF16), row(norm_post[l]))

    zero_hg = jnp.zeros((1, HG_HEADS, LANES, LANES), F32)
    zero_rw = jnp.zeros((1, RW_PAIRS, LANES, LANES), F32)
    zero_sh = jnp.zeros((1, RW_BLK, 1, LANES), F32)
    _, hg_m, rw_m, sh_m = _trunk(meta_tokens[None].astype(F32), zero_hg, zero_rw, zero_sh, weights,
                                 C=n_meta, need_y=False)

    rep = lambda t: jnp.broadcast_to(t, (B,) + t.shape[1:])
    y_p, hg_p, rw_p, sh_p = _trunk(x_prompt, rep(hg_m), rep(rw_m), rep(sh_m), weights, C=64)

    hg_s0 = jnp.swapaxes(state_hgrn[l], -1, -2)
    rw_s0 = _rw_pack(state_rwkv[l])
    sh_s0 = state_shift[l].reshape(Bs, RW_BLK, 1, LANES)
    y_s, hg_s, rw_s, sh_s = _trunk(x_sample, hg_s0, rw_s0, sh_s0, weights, C=64)

    unhg = lambda t: jnp.swapaxes(t, -1, -2)[None]
    unsh = lambda t: t.reshape(t.shape[0], 1, RW_BLK * LANES)[None]
    return (y_p, y_s,
            unhg(hg_p), _rw_unpack(rw_p)[None], unsh(sh_p),
            unhg(hg_s), _rw_unpack(rw_s)[None], unsh(sh_s))
```

```python
import functools

import jax
import jax.numpy as jnp
from jax import lax
from jax.experimental import pallas as pl
from jax.experimental.pallas import tpu as pltpu

F32 = jnp.float32
BF16 = jnp.bfloat16

LANES = 128
SUB = 8
D_MODEL = 2048
HG_HEADS = 8
HG_WIDTH = 1024
RW_PAIRS = 8
RW_HEAD = 64
RW_WIDTH = 1024
P_TOTAL = 8320
NBLK = P_TOTAL // LANES
HG_BLK = 32
RW_BLK = NBLK - HG_BLK
NORM_EPS = 1e-6
RW_GN_EPS = 64e-5
KK_EPS = 1e-12
VMEM_LIMIT = 56 * 1024 * 1024


def _bf(x):
    return x.astype(BF16)


def _dot(a, b):
    return jnp.dot(_bf(a), _bf(b), preferred_element_type=F32)


def _dot_nt(a, b):
    return lax.dot_general(_bf(a), _bf(b), (((1,), (1,)), ((), ())), preferred_element_type=F32)


def _dot_tn(a, b):
    return lax.dot_general(_bf(a), _bf(b), (((0,), (0,)), ((), ())), preferred_element_type=F32)


def _split3(x):
    hi = _bf(x)
    r = x - hi.astype(F32)
    mid = _bf(r)
    lo = _bf(r - mid.astype(F32))
    return hi, mid, lo


def _dot_exact_lhs(l_bf, x):
    return jnp.dot(jnp.concatenate([l_bf, l_bf, l_bf], axis=1), jnp.concatenate(_split3(x), axis=0),
                   preferred_element_type=F32)


def _group_sum2(xa, xb, ones2_bf):
    s = jnp.dot(_bf(jnp.concatenate([xa, xb], axis=1)), ones2_bf, preferred_element_type=F32)
    return s[:, :LANES], s[:, LANES:]


def _sigmoid(x):
    return 1.0 / (1.0 + jnp.exp(-x))


LOG2E = 1.4426950408889634


def _inproj_kernel(x_ref, g_ref, w_ref, o_ref, u_ref, *, nb):
    @pl.when(pl.program_id(1) == 0)
    def _():
        x = x_ref[...]
        ms = jnp.mean(x * x, axis=-1, keepdims=True)
        u_ref[...] = _bf(x * lax.rsqrt(ms + NORM_EPS) * g_ref[...])

    acc = jnp.dot(u_ref[...], w_ref[...], preferred_element_type=F32)
    for cb in range(nb):
        o_ref[cb] = acc[:, cb * LANES:(cb + 1) * LANES]


def _inproj(x, g, w_bf, tm):
    rows = x.shape[0]
    nb = 13
    tn = nb * LANES
    return pl.pallas_call(
        functools.partial(_inproj_kernel, nb=nb),
        grid=(rows // tm, NBLK // nb),
        in_specs=[
            pl.BlockSpec((tm, D_MODEL), lambda i, j: (i, 0)),
            pl.BlockSpec((1, D_MODEL), lambda i, j: (0, 0)),
            pl.BlockSpec((D_MODEL, tn), lambda i, j: (0, j)),
        ],
        out_specs=pl.BlockSpec((nb, tm, LANES), lambda i, j: (j, i, 0)),
        out_shape=jax.ShapeDtypeStruct((NBLK, rows, LANES), F32),
        scratch_shapes=[pltpu.VMEM((tm, D_MODEL), BF16)],
        compiler_params=pltpu.CompilerParams(
            dimension_semantics=("arbitrary", "arbitrary"), vmem_limit_bytes=VMEM_LIMIT),
        name="inproj",
    )(x, g, w_bf)


def _mixer_kernel(proj_ref, hg0_ref, rw0_ref, sh0_ref, lb_ref, hgn_ref, mu_ref, w0_ref, a0_ref,
                  w2a_ref, kk_ref, ka_ref, rk_ref, lnw_ref, lnb_ref, wsel_ref,
                  out_ref, hgS_ref, rwS_ref, sh_ref,
                  q_s, k_s, g_s, lk_s, gk_s, lw_s, gam_s, cum_s, lora_s, *, C):
    c = pl.program_id(1)

    @pl.when(c == 0)
    def _():
        hgS_ref[...] = hg0_ref[...]
        rwS_ref[...] = rw0_ref[...]
        sh_ref[...] = sh0_ref[...]

    row = lax.broadcasted_iota(jnp.int32, (C, LANES), 0)
    lane = lax.broadcasted_iota(jnp.int32, (C, LANES), 1)
    ri = lax.broadcasted_iota(jnp.int32, (C, C), 0)
    ci = lax.broadcasted_iota(jnp.int32, (C, C), 1)
    tril_bf = (ci <= ri).astype(F32).astype(BF16)

    levels = [b for b in (8, 16, 32) if 2 * b <= C]
    cum_mats = [tril_bf] + [(ci <= (ri & (-2 * b)) + b - 1).astype(F32).astype(BF16) for b in levels]

    for h in range(HG_HEADS):
        hs = slice(h * LANES, (h + 1) * LANES)
        pq = proj_ref[h]
        lb = lb_ref[:, hs]
        forget = lb + (1.0 - lb) * _sigmoid(proj_ref[HG_HEADS + h])
        k = 1.0 - forget
        q_s[:, hs] = pq * _sigmoid(pq)
        k_s[:, hs] = k
        lk_s[:, hs] = jnp.log(k) * LOG2E
        g_s[:, hs] = jnp.log(forget) * LOG2E
    cum_s[...] = _dot_exact_lhs(jnp.concatenate(cum_mats, axis=0), g_s[...])

    def two_rows(ref, base, hs):
        return jnp.concatenate([jnp.broadcast_to(ref[base:base + 1, hs], (SUB, LANES)),
                                jnp.broadcast_to(ref[base + SUB:base + SUB + 1, hs], (SUB, LANES))], axis=0)

    scores_off = [None] * HG_HEADS
    diag_lhs = [None] * HG_HEADS

    def hg_scores(h):
        hs = slice(h * LANES, (h + 1) * LANES)
        q = q_s[:, hs]
        k = k_s[:, hs]
        lk = lk_s[:, hs]
        G = cum_s[0:C, hs]
        gk_s[:, hs] = G - lk
        scores = jnp.zeros((C, C), F32)
        for n, b in enumerate(levels):
            right = (row & b) != 0
            d = G - cum_s[(n + 1) * C:(n + 2) * C, hs]
            e = jnp.exp2(jnp.minimum(jnp.where(right, d, -d), 0.0))
            sc = _dot_nt(jnp.where(right, q * e, 0.0), jnp.where(right, 0.0, k * e))
            scores += jnp.where((ri & (-2 * b)) == (ci & (-2 * b)), sc, 0.0)
        scores_off[h] = scores

        units = []
        for u in range(C // (2 * SUB)):
            r0 = u * 2 * SUB
            G16 = G[r0:r0 + 2 * SUB]
            q16 = _bf(q[r0:r0 + 2 * SUB])
            cols = []
            for j in range(SUB):
                x = jnp.minimum(G16 - two_rows(gk_s, r0 + j, hs), two_rows(lk_s, r0 + j, hs))
                cols.append(q16 * _bf(jnp.exp2(x)))
            units.append(jnp.concatenate(cols, axis=1))
        diag_lhs[h] = jnp.concatenate(units, axis=0)

    diag_keep = ((lane >> 3) == (row >> 3)) & (lane <= row)

    def hg_finish(h, diag):
        hs = slice(h * LANES, (h + 1) * LANES)
        G = cum_s[0:C, hs]
        v = proj_ref[2 * HG_HEADS + h]
        pg = proj_ref[3 * HG_HEADS + h]
        St = hgS_ref[0, h]
        scores = scores_off[h] + jnp.where(diag_keep, diag[h * C:(h + 1) * C], 0.0)[:, :C]
        o = _dot_nt(q_s[:, hs] * jnp.exp2(G), St) + _dot(scores, v)

        Gl = G[C - 1:C, :]
        hgS_ref[0, h] = St * jnp.exp2(Gl) + _dot_tn(v, jnp.exp2(Gl - gk_s[:, hs]))

        o = o * lax.rsqrt(jnp.mean(o * o, axis=-1, keepdims=True) + NORM_EPS)
        o = o * hgn_ref[:, hs] * (pg * _sigmoid(pg))
        out_ref[:, hs] = o.astype(out_ref.dtype)

    def shifted(blk):
        p = proj_ref[HG_BLK + blk]
        prev = jnp.where(row == 0, sh_ref[0, blk], pltpu.roll(p, 1, axis=0))
        return p + (prev - p) * mu_ref[blk]

    xl = shifted(RW_BLK - 1)
    xl = jnp.where(lane < RW_HEAD, jnp.tanh(xl), xl)
    lora_s[...] = jnp.dot(_bf(xl), w2a_ref[...], preferred_element_type=F32)

    head0 = lane < RW_HEAD
    blockdiag = ((lax.broadcasted_iota(jnp.int32, (LANES, LANES), 0) < RW_HEAD)
                 == (lax.broadcasted_iota(jnp.int32, (LANES, LANES), 1) < RW_HEAD))
    ones2_bf = ((lax.broadcasted_iota(jnp.int32, (2 * LANES, 2 * LANES), 0) >> 6)
                == (lax.broadcasted_iota(jnp.int32, (2 * LANES, 2 * LANES), 1) >> 6)).astype(F32).astype(BF16)
    C2 = 2 * C
    aligned = C2 % LANES == 0
    ri2 = lax.broadcasted_iota(jnp.int32, (C, C2), 0)
    ci2 = lax.broadcasted_iota(jnp.int32, (C, C2), 1)
    cj2 = jnp.where(ci2 >= C, ci2 - C, ci2)
    smask = cj2 < ri2
    imask = cj2 <= ri2
    first = ci2 < C
    eye2 = (lax.broadcasted_iota(jnp.int32, (C2, C2), 0)
            == lax.broadcasted_iota(jnp.int32, (C2, C2), 1)).astype(F32)
    n_dbl = C.bit_length() - 1

    def by_head(x):
        return jnp.concatenate([jnp.where(head0, x, 0.0), jnp.where(head0, 0.0, x)], axis=0)

    pairs = range(RW_PAIRS)
    sls = [slice(p * LANES, (p + 1) * LANES) for p in pairs]
    st = [dict() for _ in pairs]
    duos = [(st[i], st[i + 1]) for i in range(0, RW_PAIRS, 2)]

    for p, d in zip(pairs, st):
        sl = sls[p]
        d["r"] = shifted(p)
        d["xk"] = shifted(RW_PAIRS + p)
        d["v"] = shifted(2 * RW_PAIRS + p)
        w = w0_ref[:, sl] + lora_s[:, sl]
        lw_s[:, sl] = (-0.6065306597126334 * LOG2E) * _sigmoid(w)
        d["a"] = _sigmoid(a0_ref[:, sl] + lora_s[:, RW_WIDTH + p * LANES:RW_WIDTH + (p + 1) * LANES])
        d["kkr"] = d["xk"] * kk_ref[:, sl]
        d["k"] = d["xk"] * (1.0 + (d["a"] - 1.0) * ka_ref[:, sl])
    for da, db in duos:
        for d, ss in zip((da, db), _group_sum2(da["kkr"] * da["kkr"], db["kkr"] * db["kkr"], ones2_bf)):
            d["kk"] = d["kkr"] / jnp.maximum(jnp.sqrt(ss), KK_EPS)
            d["b"] = d["kk"] * d["a"]
    for p, (da, db) in enumerate(duos):
        da["bonus"], db["bonus"] = _group_sum2(da["r"] * da["k"] * rk_ref[:, sls[2 * p]],
                                               db["r"] * db["k"] * rk_ref[:, sls[2 * p + 1]], ones2_bf)

    gam_s[...] = _dot_exact_lhs(tril_bf, lw_s[...])

    for p, d in zip(pairs, st):
        sl = sls[p]
        gam = gam_s[:, sl]
        eg = jnp.exp2(gam)
        eng = jnp.exp2(-gam)
        at = -d["kk"] * jnp.exp2(gam - lw_s[:, sl])
        rt = d["r"] * eg
        bt = d["b"] * eng
        kt = d["k"] * eng
        dl = eg[C - 1:C, :] * eng
        d["bk_dl"] = jnp.concatenate([d["b"] * dl, d["k"] * dl], axis=0)
        d["eg_last"] = eg[C - 1:C, :]
        d["S"] = rwS_ref[0, p]
        at_rt = jnp.concatenate([at, rt], axis=0)
        rhs0 = jnp.concatenate([jnp.where(head0, bt, 0.0), jnp.where(head0, kt, 0.0)], axis=0)
        rhs1 = jnp.concatenate([jnp.where(head0, 0.0, kt), jnp.where(head0, 0.0, bt)], axis=0)
        if aligned:
            m = _dot_nt(at_rt, jnp.concatenate([rhs0, rhs1, d["S"]], axis=0))
            m0, m1, d["wst"] = m[:, :C2], m[:, C2:2 * C2], m[:, 2 * C2:]
        else:
            m0, m1, d["wst"] = _dot_nt(at_rt, rhs0), _dot_nt(at_rt, rhs1), _dot_nt(at_rt, d["S"])
        d["aa0"] = jnp.where(smask, m0[:C], 0.0)
        d["ar0"] = jnp.where(imask, m0[C:], 0.0)
        d["aa1"] = jnp.where(smask, m1[:C], 0.0)
        d["ar1"] = jnp.where(imask, m1[C:], 0.0)
        d["pw"] = jnp.concatenate([jnp.where(first, d["aa0"], 0.0), jnp.where(first, 0.0, d["aa1"])], axis=0)
        d["t"] = eye2 + d["pw"]

    hg_pending = list(range(HG_HEADS))

    def hg_some(n):
        for _ in range(min(n, len(hg_pending))):
            hg_scores(hg_pending.pop(0))

    for d in st:
        d["pw"] = _dot(d["pw"], d["pw"])
    hg_some(2)
    for _ in range(n_dbl - 2):
        for d in st:
            if aligned:
                res = _dot(d["pw"], jnp.concatenate([d["pw"], d["t"]], axis=1))
                d["pw"], d["t"] = res[:, :C2], d["t"] + res[:, C2:]
            else:
                d["pw"], d["t"] = _dot(d["pw"], d["pw"]), d["t"] + _dot(d["pw"], d["t"])
        hg_some(2)
    hg_some(HG_HEADS)
    diag = jnp.dot(jnp.concatenate(diag_lhs, axis=0), wsel_ref[...], preferred_element_type=F32)

    for d in st:
        akv = _dot(jnp.where(first, d["aa1"], d["aa0"]),
                   jnp.concatenate([jnp.where(head0, 0.0, d["v"]), jnp.where(head0, d["v"], 0.0)], axis=0))
        d["rhs2"] = by_head(d["wst"][:C] + akv)
    for h in range(0, HG_HEADS // 2):
        hg_finish(h, diag)
    for d in st:
        d["w"] = _dot(d["t"][:C] + d["t"][C:], d["rhs2"])
    for h in range(HG_HEADS // 2, HG_HEADS):
        hg_finish(h, diag)
    for d in st:
        d["u"] = d["w"] + _dot(d["pw"][:C] + d["pw"][C:], by_head(d["w"]))

    for p, d in zip(pairs, st):
        u, v = d["u"], d["v"]
        if aligned:
            yuv = _dot(jnp.concatenate([d["ar0"], d["ar1"]], axis=1),
                       jnp.concatenate([jnp.where(head0, u, 0.0), jnp.where(head0, v, 0.0),
                                        jnp.where(head0, 0.0, v), jnp.where(head0, 0.0, u)], axis=0))
        else:
            yuv = (_dot(d["ar0"], jnp.concatenate([jnp.where(head0, u, 0.0), jnp.where(head0, v, 0.0)], axis=0))
                   + _dot(d["ar1"], jnp.concatenate([jnp.where(head0, 0.0, v), jnp.where(head0, 0.0, u)], axis=0)))
        d["y"] = d["wst"][C:] + yuv
        upd = _dot_tn(jnp.concatenate([u, v], axis=0), d["bk_dl"])
        rwS_ref[0, p] = d["S"] * d["eg_last"] + jnp.where(blockdiag, upd, 0.0)

    for da, db in duos:
        ma, mb = _group_sum2(da["y"], db["y"], ones2_bf)
        da["yc"] = da["y"] - ma * (1.0 / RW_HEAD)
        db["yc"] = db["y"] - mb * (1.0 / RW_HEAD)
    for da, db in duos:
        da["var"], db["var"] = _group_sum2(da["yc"] * da["yc"], db["yc"] * db["yc"], ones2_bf)
    for p, d in zip(pairs, st):
        sl = sls[p]
        yn = d["yc"] * lax.rsqrt(d["var"] * (1.0 / RW_HEAD) + RW_GN_EPS) * lnw_ref[:, sl] + lnb_ref[:, sl]
        xg = shifted(3 * RW_PAIRS + p)
        yo = (yn + d["bonus"] * d["v"]) * (xg * _sigmoid(xg))
        out_ref[:, HG_WIDTH + p * LANES:HG_WIDTH + (p + 1) * LANES] = yo.astype(out_ref.dtype)

    for blk in range(RW_BLK):
        sh_ref[0, blk] = proj_ref[HG_BLK + blk, C - 1:C, :]


def _mixer(proj, hg0, rw0, sh0, params, B, T, C):
    nc = T // C
    full = lambda shape: pl.BlockSpec(shape, lambda b, c: (0,) * len(shape))
    per_b = lambda shape: pl.BlockSpec((1,) + shape, lambda b, c: (b,) + (0,) * len(shape))
    lb, hgn, mu, w0, a0, w2a, kk, ka, rk, lnw, lnb = params
    wsel = ((jnp.arange(SUB * LANES)[:, None] // LANES) == (jnp.arange(LANES)[None, :] % SUB)).astype(BF16)
    n_cum = 1 + len([b for b in (8, 16, 32) if 2 * b <= C])
    in_specs = [
        pl.BlockSpec((NBLK, C, LANES), lambda b, c: (0, b * nc + c, 0)),
        per_b((HG_HEADS, LANES, LANES)),
        per_b((RW_PAIRS, LANES, LANES)),
        per_b((RW_BLK, 1, LANES)),
        full((1, HG_WIDTH)), full((1, HG_WIDTH)), full((RW_BLK, 1, LANES)),
        full((1, RW_WIDTH)), full((1, RW_WIDTH)), full((LANES, 2 * RW_WIDTH)),
        full((1, RW_WIDTH)), full((1, RW_WIDTH)), full((1, RW_WIDTH)),
        full((1, RW_WIDTH)), full((1, RW_WIDTH)), full((SUB * LANES, LANES)),
    ]
    out_specs = [
        pl.BlockSpec((C, D_MODEL), lambda b, c: (b * nc + c, 0)),
        per_b((HG_HEADS, LANES, LANES)),
        per_b((RW_PAIRS, LANES, LANES)),
        per_b((RW_BLK, 1, LANES)),
    ]
    out_shape = [
        jax.ShapeDtypeStruct((B * T, D_MODEL), BF16),
        jax.ShapeDtypeStruct((B, HG_HEADS, LANES, LANES), F32),
        jax.ShapeDtypeStruct((B, RW_PAIRS, LANES, LANES), F32),
        jax.ShapeDtypeStruct((B, RW_BLK, 1, LANES), F32),
    ]
    scratch = ([pltpu.VMEM((C, HG_WIDTH), F32)] * 7 + [pltpu.VMEM((n_cum * C, HG_WIDTH), F32)]
               + [pltpu.VMEM((C, 2 * RW_WIDTH), F32)])
    return pl.pallas_call(
        functools.partial(_mixer_kernel, C=C),
        grid=(B, nc),
        in_specs=in_specs,
        out_specs=out_specs,
        out_shape=out_shape,
        scratch_shapes=scratch,
        compiler_params=pltpu.CompilerParams(
            dimension_semantics=("arbitrary", "arbitrary"), vmem_limit_bytes=VMEM_LIMIT),
        name="mixer",
    )(proj, hg0, rw0, sh0, lb, hgn, mu, w0, a0, w2a, kk, ka, rk, lnw, lnb, wsel)


def _outproj_kernel(m_ref, w_ref, g_ref, x_ref, o_ref):
    z = jnp.dot(m_ref[...], w_ref[...], preferred_element_type=F32)
    ms = jnp.mean(z * z, axis=-1, keepdims=True)
    o_ref[...] = x_ref[...] + z * lax.rsqrt(ms + NORM_EPS) * g_ref[...]


def _outproj(mixed, w_bf, g, x, tm):
    rows = x.shape[0]
    return pl.pallas_call(
        _outproj_kernel,
        grid=(rows // tm,),
        in_specs=[
            pl.BlockSpec((tm, D_MODEL), lambda i: (i, 0)),
            pl.BlockSpec((D_MODEL, D_MODEL), lambda i: (0, 0)),
            pl.BlockSpec((1, D_MODEL), lambda i: (0, 0)),
            pl.BlockSpec((tm, D_MODEL), lambda i: (i, 0)),
        ],
        out_specs=pl.BlockSpec((tm, D_MODEL), lambda i: (i, 0)),
        out_shape=jax.ShapeDtypeStruct((rows, D_MODEL), F32),
        compiler_params=pltpu.CompilerParams(
            dimension_semantics=("arbitrary",), vmem_limit_bytes=VMEM_LIMIT),
        name="outproj",
    )(mixed, w_bf, g, x)


def _rw_pack(s):
    B = s.shape[0]
    s = s.reshape(B, RW_PAIRS, 2, RW_HEAD, RW_HEAD)
    z = jnp.zeros_like(s[:, :, 0])
    top = jnp.concatenate([s[:, :, 0], z], axis=-1)
    bot = jnp.concatenate([z, s[:, :, 1]], axis=-1)
    return jnp.concatenate([top, bot], axis=-2)


def _rw_unpack(s):
    B = s.shape[0]
    a = s[:, :, :RW_HEAD, :RW_HEAD]
    b = s[:, :, RW_HEAD:, RW_HEAD:]
    return jnp.stack([a, b], axis=2).reshape(B, 2 * RW_PAIRS, RW_HEAD, RW_HEAD)


def _pick_tile(rows, cap):
    t = cap
    while rows % t:
        t //= 2
    return t


def _trunk(x, hg0, rw0, sh0, weights, C, need_y=True):
    (norm_pre, w_in_bf, mixer_params, w_out_bf, norm_post) = weights
    B, T, _ = x.shape
    x2 = x.reshape(B * T, D_MODEL)
    proj = _inproj(x2, norm_pre, w_in_bf, _pick_tile(B * T, 1024))
    mixed, hg, rw, sh = _mixer(proj, hg0, rw0, sh0, mixer_params, B, T, C)
    y = None
    if need_y:
        y = _outproj(mixed, w_out_bf, norm_post, x2, _pick_tile(B * T, 512)).reshape(B, T, D_MODEL)
    return y, hg, rw, sh


def kernel(x_prompt, x_sample, state_hgrn, state_rwkv, state_shift, meta_tokens, norm_pre, w_in,
           hg_lower_bounds, hg_norm, rw_mu, rw_w0, rw_w2, rw_a0, rw_a2, rw_k_k, rw_k_a, rw_r_k,
           rw_ln_w, rw_ln_b, w_out, norm_post):
    B = x_prompt.shape[0]
    Bs = x_sample.shape[0]
    n_meta = meta_tokens.shape[0]
    l = 0

    lbs = jnp.cumsum(jax.nn.softmax(hg_lower_bounds.astype(F32), axis=0), axis=0)[l].reshape(1, HG_WIDTH)
    z64 = jnp.zeros((RW_HEAD, RW_WIDTH), F32)
    w2a = jnp.concatenate([jnp.concatenate([rw_w2[l], z64], axis=1),
                           jnp.concatenate([z64, rw_a2[l]], axis=1)], axis=0).astype(BF16)
    row = lambda t: t.reshape(1, -1).astype(F32)
    mixer_params = (lbs, row(hg_norm[l]), rw_mu[l].reshape(RW_BLK, 1, LANES), row(rw_w0[l]), row(rw_a0[l]),
                    w2a, row(rw_k_k[l]), row(rw_k_a[l]), row(rw_r_k[l]), row(rw_ln_w[l]), row(rw_ln_b[l]))
    weights = (row(norm_pre[l]), w_in[l].astype(BF16), mixer_params, w_out[l].astype(BF16), row(norm_post[l]))

    zero_hg = jnp.zeros((1, HG_HEADS, LANES, LANES), F32)
    zero_rw = jnp.zeros((1, RW_PAIRS, LANES, LANES), F32)
    zero_sh = jnp.zeros((1, RW_BLK, 1, LANES), F32)
    _, hg_m, rw_m, sh_m = _trunk(meta_tokens[None].astype(F32), zero_hg, zero_rw, zero_sh, weights,
                                 C=n_meta, need_y=False)

    rep = lambda t: jnp.broadcast_to(t, (B,) + t.shape[1:])
    y_p, hg_p, rw_p, sh_p = _trunk(x_prompt, rep(hg_m), rep(rw_m), rep(sh_m), weights, C=64)

    hg_s0 = jnp.swapaxes(state_hgrn[l], -1, -2)
    rw_s0 = _rw_pack(state_rwkv[l])
    sh_s0 = state_shift[l].reshape(Bs, RW_BLK, 1, LANES)
    y_s, hg_s, rw_s, sh_s = _trunk(x_sample, hg_s0, rw_s0, sh_s0, weights, C=64)

    unhg = lambda t: jnp.swapaxes(t, -1, -2)[None]
    unsh = lambda t: t.reshape(t.shape[0], 1, RW_BLK * LANES)[None]
    return (y_p, y_s,
            unhg(hg_p), _rw_unpack(rw_p)[None], unsh(sh_p),
            unhg(hg_s), _rw_unpack(rw_s)[None], unsh(sh_s))
```

```python
import functools

import jax
import jax.numpy as jnp
from jax import lax
from jax.experimental import pallas as pl
from jax.experimental.pallas import tpu as pltpu

F32 = jnp.float32
BF16 = jnp.bfloat16

LANES = 128
SUB = 8
D_MODEL = 2048
HG_HEADS = 8
HG_WIDTH = 1024
RW_PAIRS = 8
RW_HEAD = 64
RW_WIDTH = 1024
P_TOTAL = 8320
NBLK = P_TOTAL // LANES
HG_BLK = 32
RW_BLK = NBLK - HG_BLK
NORM_EPS = 1e-6
RW_GN_EPS = 64e-5
KK_EPS = 1e-12
VMEM_LIMIT = 56 * 1024 * 1024
LOG2E = 1.4426950408889634
CHUNK = 64
STREAMS_PER_STEP = 2
STAGGER = 1


def _bf(x):
    return x.astype(BF16)


def _dot(a, b):
    return jnp.dot(_bf(a), _bf(b), preferred_element_type=F32)


def _dot_nt(a, b):
    return lax.dot_general(_bf(a), _bf(b), (((1,), (1,)), ((), ())), preferred_element_type=F32)


def _dot_tn(a, b):
    return lax.dot_general(_bf(a), _bf(b), (((0,), (0,)), ((), ())), preferred_element_type=F32)


def _split3(x):
    hi = _bf(x)
    r = x - hi.astype(F32)
    mid = _bf(r)
    lo = _bf(r - mid.astype(F32))
    return hi, mid, lo


def _dot_exact_lhs(l_bf, x):
    return jnp.dot(jnp.concatenate([l_bf, l_bf, l_bf], axis=1), jnp.concatenate(_split3(x), axis=0),
                   preferred_element_type=F32)


def _group_sum2(xa, xb, ones2_bf):
    s = jnp.dot(_bf(jnp.concatenate([xa, xb], axis=1)), ones2_bf, preferred_element_type=F32)
    return s[:, :LANES], s[:, LANES:]


def _sigmoid(x):
    return 1.0 / (1.0 + jnp.exp(-x))


def _inproj_kernel(x_ref, g_ref, w_ref, o_ref, u_ref, *, nb):
    @pl.when(pl.program_id(1) == 0)
    def _():
        x = x_ref[...]
        ms = jnp.mean(x * x, axis=-1, keepdims=True)
        u_ref[...] = _bf(x * lax.rsqrt(ms + NORM_EPS) * g_ref[...])

    acc = jnp.dot(u_ref[...], w_ref[...], preferred_element_type=F32)
    for cb in range(nb):
        o_ref[cb] = acc[:, cb * LANES:(cb + 1) * LANES]


def _inproj(x, g, w_bf, tm):
    rows = x.shape[0]
    nb = 13
    tn = nb * LANES
    return pl.pallas_call(
        functools.partial(_inproj_kernel, nb=nb),
        grid=(rows // tm, NBLK // nb),
        in_specs=[
            pl.BlockSpec((tm, D_MODEL), lambda i, j: (i, 0)),
            pl.BlockSpec((1, D_MODEL), lambda i, j: (0, 0)),
            pl.BlockSpec((D_MODEL, tn), lambda i, j: (0, j)),
        ],
        out_specs=pl.BlockSpec((nb, tm, LANES), lambda i, j: (j, i, 0)),
        out_shape=jax.ShapeDtypeStruct((NBLK, rows, LANES), F32),
        scratch_shapes=[pltpu.VMEM((tm, D_MODEL), BF16)],
        compiler_params=pltpu.CompilerParams(
            dimension_semantics=("arbitrary", "arbitrary"), vmem_limit_bytes=VMEM_LIMIT),
        name="inproj",
    )(x, g, w_bf)


def _mixer_kernel(proj_ref, hg0_ref, rw0_ref, sh0_ref, lb_ref, hgn_ref, mu_ref, w0_ref, a0_ref,
                  w2a_ref, kk_ref, ka_ref, rk_ref, lnw_ref, lnb_ref, wsel_ref,
                  out_ref, hgS_ref, rwS_ref, sh_ref,
                  q_all, k_all, g_all, lk_all, gk_all, lw_all, gam_all, cum_all, lora_all, *, C, NS, stagger):
    @pl.when(pl.program_id(1) == 0)
    def _():
        hgS_ref[...] = hg0_ref[...]
        rwS_ref[...] = rw0_ref[...]
        sh_ref[...] = sh0_ref[...]

    row = lax.broadcasted_iota(jnp.int32, (C, LANES), 0)
    lane = lax.broadcasted_iota(jnp.int32, (C, LANES), 1)
    ri = lax.broadcasted_iota(jnp.int32, (C, C), 0)
    ci = lax.broadcasted_iota(jnp.int32, (C, C), 1)
    tril_bf = (ci <= ri).astype(F32).astype(BF16)
    levels = [b for b in (8, 16, 32) if 2 * b <= C]
    cum_mats = [tril_bf] + [(ci <= (ri & (-2 * b)) + b - 1).astype(F32).astype(BF16) for b in levels]
    diag_keep = ((lane >> 3) == (row >> 3)) & (lane <= row)
    head0 = lane < RW_HEAD
    blockdiag = ((lax.broadcasted_iota(jnp.int32, (LANES, LANES), 0) < RW_HEAD)
                 == (lax.broadcasted_iota(jnp.int32, (LANES, LANES), 1) < RW_HEAD))
    ones2_bf = ((lax.broadcasted_iota(jnp.int32, (2 * LANES, 2 * LANES), 0) >> 6)
                == (lax.broadcasted_iota(jnp.int32, (2 * LANES, 2 * LANES), 1) >> 6)).astype(F32).astype(BF16)
    C2 = 2 * C
    aligned = C2 % LANES == 0
    ri2 = lax.broadcasted_iota(jnp.int32, (C, C2), 0)
    ci2 = lax.broadcasted_iota(jnp.int32, (C, C2), 1)
    cj2 = jnp.where(ci2 >= C, ci2 - C, ci2)
    smask = cj2 < ri2
    imask = cj2 <= ri2
    first = ci2 < C
    eye2 = (lax.broadcasted_iota(jnp.int32, (C2, C2), 0)
            == lax.broadcasted_iota(jnp.int32, (C2, C2), 1)).astype(F32)
    n_dbl = C.bit_length() - 1
    half = HG_HEADS // 2
    pairs = range(RW_PAIRS)
    sls = [slice(p * LANES, (p + 1) * LANES) for p in pairs]

    def by_head(x):
        return jnp.concatenate([jnp.where(head0, x, 0.0), jnp.where(head0, 0.0, x)], axis=0)

    def two_rows(ref, base, hs):
        return jnp.concatenate([jnp.broadcast_to(ref[base:base + 1, hs], (SUB, LANES)),
                                jnp.broadcast_to(ref[base + SUB:base + SUB + 1, hs], (SUB, LANES))], axis=0)

    def program(s):
        proj = proj_ref.at[:, s]
        out = out_ref.at[s]
        hgS, rwS, sh = hgS_ref.at[s], rwS_ref.at[s], sh_ref.at[s]
        q_s, k_s, g_s, lk_s, gk_s = q_all.at[s], k_all.at[s], g_all.at[s], lk_all.at[s], gk_all.at[s]
        lw_s, gam_s, cum_s, lora_s = lw_all.at[s], gam_all.at[s], cum_all.at[s], lora_all.at[s]

        def hg_gates(_):
            for h in range(HG_HEADS):
                hs = slice(h * LANES, (h + 1) * LANES)
                pq = proj[h]
                lb = lb_ref[:, hs]
                forget = lb + (1.0 - lb) * _sigmoid(proj[HG_HEADS + h])
                k = 1.0 - forget
                q_s[:, hs] = pq * _sigmoid(pq)
                k_s[:, hs] = k
                lk_s[:, hs] = jnp.log(k) * LOG2E
                g_s[:, hs] = jnp.log(forget) * LOG2E
            cum_s[...] = _dot_exact_lhs(jnp.concatenate(cum_mats, axis=0), g_s[...])

        scores_off = [None] * HG_HEADS
        diag_lhs = [None] * HG_HEADS
        diags = {}

        def hg_scores(h):
            hs = slice(h * LANES, (h + 1) * LANES)
            q = q_s[:, hs]
            k = k_s[:, hs]
            lk = lk_s[:, hs]
            G = cum_s[0:C, hs]
            gk_s[:, hs] = G - lk
            scores = jnp.zeros((C, C), F32)
            for n, b in enumerate(levels):
                right = (row & b) != 0
                d = G - cum_s[(n + 1) * C:(n + 2) * C, hs]
                e = jnp.exp2(jnp.minimum(jnp.where(right, d, -d), 0.0))
                sc = _dot_nt(jnp.where(right, q * e, 0.0), jnp.where(right, 0.0, k * e))
                scores += jnp.where((ri & (-2 * b)) == (ci & (-2 * b)), sc, 0.0)
            scores_off[h] = scores

            units = []
            for u in range(C // (2 * SUB)):
                r0 = u * 2 * SUB
                G16 = G[r0:r0 + 2 * SUB]
                q16 = _bf(q[r0:r0 + 2 * SUB])
                cols = []
                for j in range(SUB):
                    x = jnp.minimum(G16 - two_rows(gk_s, r0 + j, hs), two_rows(lk_s, r0 + j, hs))
                    cols.append(q16 * _bf(jnp.exp2(x)))
                units.append(jnp.concatenate(cols, axis=1))
            diag_lhs[h] = jnp.concatenate(units, axis=0)

        def hg_diag(part):
            lhs = jnp.concatenate(diag_lhs[part * half:(part + 1) * half], axis=0)
            diags[part] = jnp.dot(lhs, wsel_ref[...], preferred_element_type=F32)

        def hg_finish(h):
            hs = slice(h * LANES, (h + 1) * LANES)
            diag_h = diags[h // half][(h % half) * C:(h % half + 1) * C]
            G = cum_s[0:C, hs]
            v = proj[2 * HG_HEADS + h]
            pg = proj[3 * HG_HEADS + h]
            St = hgS[h]
            scores = scores_off[h] + jnp.where(diag_keep, diag_h, 0.0)[:, :C]
            o = _dot_nt(q_s[:, hs] * jnp.exp2(G), St) + _dot(scores, v)

            Gl = G[C - 1:C, :]
            hgS[h] = St * jnp.exp2(Gl) + _dot_tn(v, jnp.exp2(Gl - gk_s[:, hs]))

            o = o * lax.rsqrt(jnp.mean(o * o, axis=-1, keepdims=True) + NORM_EPS)
            o = o * hgn_ref[:, hs] * (pg * _sigmoid(pg))
            out[:, hs] = o.astype(out.dtype)

        S_, F_, D_ = hg_scores, hg_finish, hg_diag
        fillers = [(hg_gates, 0), (S_, 0), (S_, 1), (S_, 2), (S_, 3), (D_, 0), (S_, 4), (S_, 5), (F_, 0), (F_, 1),
                   (S_, 6), (F_, 2), (F_, 3), (S_, 7), (D_, 1), (F_, 4), (F_, 5), (F_, 6), (F_, 7)]

        def fill(n):
            for _ in range(min(n, len(fillers))):
                fn, arg = fillers.pop(0)
                fn(arg)

        def shifted(blk):
            p = proj[HG_BLK + blk]
            prev = jnp.where(row == 0, sh[blk], pltpu.roll(p, 1, axis=0))
            return p + (prev - p) * mu_ref[blk]

        xl = shifted(RW_BLK - 1)
        xl = jnp.where(lane < RW_HEAD, jnp.tanh(xl), xl)
        lora_s[...] = jnp.dot(_bf(xl), w2a_ref[...], preferred_element_type=F32)

        st = [dict() for _ in pairs]
        duos = [(st[i], st[i + 1]) for i in range(0, RW_PAIRS, 2)]

        for p, d in zip(pairs, st):
            sl = sls[p]
            d["r"] = shifted(p)
            d["xk"] = shifted(RW_PAIRS + p)
            d["v"] = shifted(2 * RW_PAIRS + p)
            w = w0_ref[:, sl] + lora_s[:, sl]
            lw_s[:, sl] = (-0.6065306597126334 * LOG2E) * _sigmoid(w)
            d["a"] = _sigmoid(a0_ref[:, sl] + lora_s[:, RW_WIDTH + p * LANES:RW_WIDTH + (p + 1) * LANES])
            d["kkr"] = d["xk"] * kk_ref[:, sl]
            d["k"] = d["xk"] * (1.0 + (d["a"] - 1.0) * ka_ref[:, sl])
        yield
        for da, db in duos:
            for d, ss in zip((da, db), _group_sum2(da["kkr"] * da["kkr"], db["kkr"] * db["kkr"], ones2_bf)):
                d["kk"] = d["kkr"] / jnp.maximum(jnp.sqrt(ss), KK_EPS)
                d["b"] = d["kk"] * d["a"]
        for p, (da, db) in enumerate(duos):
            da["bonus"], db["bonus"] = _group_sum2(da["r"] * da["k"] * rk_ref[:, sls[2 * p]],
                                                   db["r"] * db["k"] * rk_ref[:, sls[2 * p + 1]], ones2_bf)

        gam_s[...] = _dot_exact_lhs(tril_bf, lw_s[...])
        yield

        for p, d in zip(pairs, st):
            sl = sls[p]
            gam = gam_s[:, sl]
            eg = jnp.exp2(gam)
            eng = jnp.exp2(-gam)
            at = -d["kk"] * jnp.exp2(gam - lw_s[:, sl])
            rt = d["r"] * eg
            bt = d["b"] * eng
            kt = d["k"] * eng
            dl = eg[C - 1:C, :] * eng
            d["bk_dl"] = _bf(jnp.concatenate([d["b"] * dl, d["k"] * dl], axis=0))
            d["eg_last"] = eg[C - 1:C, :]
            d["S"] = rwS[p]
            at_rt = jnp.concatenate([at, rt], axis=0)
            rhs0 = jnp.concatenate([jnp.where(head0, bt, 0.0), jnp.where(head0, kt, 0.0)], axis=0)
            rhs1 = jnp.concatenate([jnp.where(head0, 0.0, kt), jnp.where(head0, 0.0, bt)], axis=0)
            if aligned:
                m = _dot_nt(at_rt, jnp.concatenate([rhs0, rhs1, d["S"]], axis=0))
                m0, m1, d["wst"] = m[:, :C2], m[:, C2:2 * C2], m[:, 2 * C2:]
            else:
                m0, m1, d["wst"] = _dot_nt(at_rt, rhs0), _dot_nt(at_rt, rhs1), _dot_nt(at_rt, d["S"])
            aa0 = jnp.where(smask, m0[:C], 0.0)
            aa1 = jnp.where(smask, m1[:C], 0.0)
            d["ak"] = _bf(jnp.where(first, aa1, aa0))
            d["ar0"] = _bf(jnp.where(imask, m0[C:], 0.0))
            d["ar1"] = _bf(jnp.where(imask, m1[C:], 0.0))
            pw = jnp.concatenate([jnp.where(first, aa0, 0.0), jnp.where(first, 0.0, aa1)], axis=0)
            d["t"] = eye2 + pw
            d["pw"] = _bf(pw)
        fill(1)
        yield

        for d in st:
            d["pw"] = _bf(_dot(d["pw"], d["pw"]))
        fill(1)
        yield
        for _ in range(n_dbl - 2):
            for d in st:
                if aligned:
                    res = _dot(d["pw"], jnp.concatenate([d["pw"], _bf(d["t"])], axis=1))
                    d["pw"], d["t"] = _bf(res[:, :C2]), d["t"] + res[:, C2:]
                else:
                    d["pw"], d["t"] = _bf(_dot(d["pw"], d["pw"])), d["t"] + _dot(d["pw"], d["t"])
            fill(1)
            yield

        for d in st:
            akv = _dot(d["ak"],
                       jnp.concatenate([jnp.where(head0, 0.0, d["v"]), jnp.where(head0, d["v"], 0.0)], axis=0))
            d["rhs2"] = _bf(by_head(d["wst"][:C] + akv))
        fill(1)
        yield
        for d in st:
            d["w"] = _dot(d["t"][:C] + d["t"][C:], d["rhs2"])
        fill(3)
        yield
        for d in st:
            d["u"] = d["w"] + _dot(d["pw"][:C] + d["pw"][C:], by_head(d["w"]))
        fill(3)
        yield

        for p, d in zip(pairs, st):
            u, v = d["u"], d["v"]
            if aligned:
                yuv = _dot(jnp.concatenate([d["ar0"], d["ar1"]], axis=1),
                           jnp.concatenate([jnp.where(head0, u, 0.0), jnp.where(head0, v, 0.0),
                                            jnp.where(head0, 0.0, v), jnp.where(head0, 0.0, u)], axis=0))
            else:
                yuv = (_dot(d["ar0"], jnp.concatenate([jnp.where(head0, u, 0.0), jnp.where(head0, v, 0.0)], axis=0))
                       + _dot(d["ar1"], jnp.concatenate([jnp.where(head0, 0.0, v), jnp.where(head0, 0.0, u)], axis=0)))
            d["y"] = d["wst"][C:] + yuv
            upd = _dot_tn(jnp.concatenate([u, v], axis=0), d["bk_dl"])
            rwS[p] = d["S"] * d["eg_last"] + jnp.where(blockdiag, upd, 0.0)
        fill(1)
        yield

        for da, db in duos:
            ma, mb = _group_sum2(da["y"], db["y"], ones2_bf)
            da["yc"] = da["y"] - ma * (1.0 / RW_HEAD)
            db["yc"] = db["y"] - mb * (1.0 / RW_HEAD)
        fill(1)
        yield
        for da, db in duos:
            da["var"], db["var"] = _group_sum2(da["yc"] * da["yc"], db["yc"] * db["yc"], ones2_bf)
        fill(2)
        yield
        for p, d in zip(pairs, st):
            sl = sls[p]
            yn = d["yc"] * lax.rsqrt(d["var"] * (1.0 / RW_HEAD) + RW_GN_EPS) * lnw_ref[:, sl] + lnb_ref[:, sl]
            xg = shifted(3 * RW_PAIRS + p)
            yo = (yn + d["bonus"] * d["v"]) * (xg * _sigmoid(xg))
            out[:, HG_WIDTH + p * LANES:HG_WIDTH + (p + 1) * LANES] = yo.astype(out.dtype)
        fill(len(fillers))

        for blk in range(RW_BLK):
            sh[blk] = proj[HG_BLK + blk, C - 1:C, :]

    live = [program(s) for s in range(NS)]
    for _ in range(stagger if NS > 1 else 0):
        next(live[0], None)
    while live:
        live = [g for g in live if next(g, StopIteration) is not StopIteration]


def _mixer(proj, hg0, rw0, sh0, params, B, T, C):
    nc = T // C
    NS = STREAMS_PER_STEP if B % STREAMS_PER_STEP == 0 else 1
    full = lambda shape: pl.BlockSpec(shape, lambda b, c: (0,) * len(shape))
    per_b = lambda shape: pl.BlockSpec((NS,) + shape, lambda b, c: (b,) + (0,) * len(shape))
    lb, hgn, mu, w0, a0, w2a, kk, ka, rk, lnw, lnb = params
    wsel = ((jnp.arange(SUB * LANES)[:, None] // LANES) == (jnp.arange(LANES)[None, :] % SUB)).astype(BF16)
    n_cum = 1 + len([b for b in (8, 16, 32) if 2 * b <= C])
    in_specs = [
        pl.BlockSpec((NBLK, NS, C, LANES), lambda b, c: (0, b, c, 0)),
        per_b((HG_HEADS, LANES, LANES)),
        per_b((RW_PAIRS, LANES, LANES)),
        per_b((RW_BLK, 1, LANES)),
        full((1, HG_WIDTH)), full((1, HG_WIDTH)), full((RW_BLK, 1, LANES)),
        full((1, RW_WIDTH)), full((1, RW_WIDTH)), full((LANES, 2 * RW_WIDTH)),
        full((1, RW_WIDTH)), full((1, RW_WIDTH)), full((1, RW_WIDTH)),
        full((1, RW_WIDTH)), full((1, RW_WIDTH)), full((SUB * LANES, LANES)),
    ]
    out_specs = [
        pl.BlockSpec((NS, C, D_MODEL), lambda b, c: (b, c, 0)),
        per_b((HG_HEADS, LANES, LANES)),
        per_b((RW_PAIRS, LANES, LANES)),
        per_b((RW_BLK, 1, LANES)),
    ]
    out_shape = [
        jax.ShapeDtypeStruct((B, T, D_MODEL), BF16),
        jax.ShapeDtypeStruct((B, HG_HEADS, LANES, LANES), F32),
        jax.ShapeDtypeStruct((B, RW_PAIRS, LANES, LANES), F32),
        jax.ShapeDtypeStruct((B, RW_BLK, 1, LANES), F32),
    ]
    scratch = ([pltpu.VMEM((NS, C, HG_WIDTH), F32)] * 7 + [pltpu.VMEM((NS, n_cum * C, HG_WIDTH), F32)]
               + [pltpu.VMEM((NS, C, 2 * RW_WIDTH), F32)])
    mixed, hg, rw, sh = pl.pallas_call(
        functools.partial(_mixer_kernel, C=C, NS=NS, stagger=STAGGER),
        grid=(B // NS, nc),
        in_specs=in_specs,
        out_specs=out_specs,
        out_shape=out_shape,
        scratch_shapes=scratch,
        compiler_params=pltpu.CompilerParams(
            dimension_semantics=("arbitrary", "arbitrary"), vmem_limit_bytes=VMEM_LIMIT),
        name="mixer",
    )(proj.reshape(NBLK, B, T, LANES), hg0, rw0, sh0, lb, hgn, mu, w0, a0, w2a, kk, ka, rk, lnw, lnb, wsel)
    return mixed.reshape(B * T, D_MODEL), hg, rw, sh


def _outproj_kernel(m_ref, w_ref, g_ref, x_ref, o_ref):
    z = jnp.dot(m_ref[...], w_ref[...], preferred_element_type=F32)
    ms = jnp.mean(z * z, axis=-1, keepdims=True)
    o_ref[...] = x_ref[...] + z * lax.rsqrt(ms + NORM_EPS) * g_ref[...]


def _outproj(mixed, w_bf, g, x, tm):
    rows = x.shape[0]
    return pl.pallas_call(
        _outproj_kernel,
        grid=(rows // tm,),
        in_specs=[
            pl.BlockSpec((tm, D_MODEL), lambda i: (i, 0)),
            pl.BlockSpec((D_MODEL, D_MODEL), lambda i: (0, 0)),
            pl.BlockSpec((1, D_MODEL), lambda i: (0, 0)),
            pl.BlockSpec((tm, D_MODEL), lambda i: (i, 0)),
        ],
        out_specs=pl.BlockSpec((tm, D_MODEL), lambda i: (i, 0)),
        out_shape=jax.ShapeDtypeStruct((rows, D_MODEL), F32),
        compiler_params=pltpu.CompilerParams(
            dimension_semantics=("arbitrary",), vmem_limit_bytes=VMEM_LIMIT),
        name="outproj",
    )(mixed, w_bf, g, x)


def _rw_pack(s):
    B = s.shape[0]
    s = s.reshape(B, RW_PAIRS, 2, RW_HEAD, RW_HEAD)
    z = jnp.zeros_like(s[:, :, 0])
    top = jnp.concatenate([s[:, :, 0], z], axis=-1)
    bot = jnp.concatenate([z, s[:, :, 1]], axis=-1)
    return jnp.concatenate([top, bot], axis=-2)


def _rw_unpack(s):
    B = s.shape[0]
    a = s[:, :, :RW_HEAD, :RW_HEAD]
    b = s[:, :, RW_HEAD:, RW_HEAD:]
    return jnp.stack([a, b], axis=2).reshape(B, 2 * RW_PAIRS, RW_HEAD, RW_HEAD)


def _pick_tile(rows, cap):
    t = cap
    while rows % t:
        t //= 2
    return t


def _trunk(x, hg0, rw0, sh0, weights, C, need_y=True):
    (norm_pre, w_in_bf, mixer_params, w_out_bf, norm_post) = weights
    B, T, _ = x.shape
    x2 = x.reshape(B * T, D_MODEL)
    proj = _inproj(x2, norm_pre, w_in_bf, _pick_tile(B * T, 1024))
    mixed, hg, rw, sh = _mixer(proj, hg0, rw0, sh0, mixer_params, B, T, C)
    y = None
    if need_y:
        y = _outproj(mixed, w_out_bf, norm_post, x2, _pick_tile(B * T, 512)).reshape(B, T, D_MODEL)
    return y, hg, rw, sh


def kernel(x_prompt, x_sample, state_hgrn, state_rwkv, state_shift, meta_tokens, norm_pre, w_in,
           hg_lower_bounds, hg_norm, rw_mu, rw_w0, rw_w2, rw_a0, rw_a2, rw_k_k, rw_k_a, rw_r_k,
           rw_ln_w, rw_ln_b, w_out, norm_post):
    B = x_prompt.shape[0]
    Bs = x_sample.shape[0]
    n_meta = meta_tokens.shape[0]
    l = 0

    lbs = jnp.cumsum(jax.nn.softmax(hg_lower_bounds.astype(F32), axis=0), axis=0)[l].reshape(1, HG_WIDTH)
    z64 = jnp.zeros((RW_HEAD, RW_WIDTH), F32)
    w2a = jnp.concatenate([jnp.concatenate([rw_w2[l], z64], axis=1),
                           jnp.concatenate([z64, rw_a2[l]], axis=1)], axis=0).astype(BF16)
    row = lambda t: t.reshape(1, -1).astype(F32)
    mixer_params = (lbs, row(hg_norm[l]), rw_mu[l].reshape(RW_BLK, 1, LANES), row(rw_w0[l]), row(rw_a0[l]),
                    w2a, row(rw_k_k[l]), row(rw_k_a[l]), row(rw_r_k[l]), row(rw_ln_w[l]), row(rw_ln_b[l]))
    weights = (row(norm_pre[l]), w_in[l].astype(BF16), mixer_params, w_out[l].astype(BF16), row(norm_post[l]))

    zero_hg = jnp.zeros((1, HG_HEADS, LANES, LANES), F32)
    zero_rw = jnp.zeros((1, RW_PAIRS, LANES, LANES), F32)
    zero_sh = jnp.zeros((1, RW_BLK, 1, LANES), F32)
    _, hg_m, rw_m, sh_m = _trunk(meta_tokens[None].astype(F32), zero_hg, zero_rw, zero_sh, weights,
                                 C=n_meta, need_y=False)

    rep = lambda t: jnp.broadcast_to(t, (B,) + t.shape[1:])
    y_p, hg_p, rw_p, sh_p = _trunk(x_prompt, rep(hg_m), rep(rw_m), rep(sh_m), weights, C=CHUNK)

    hg_s0 = jnp.swapaxes(state_hgrn[l], -1, -2)
    rw_s0 = _rw_pack(state_rwkv[l])
    sh_s0 = state_shift[l].reshape(Bs, RW_BLK, 1, LANES)
    y_s, hg_s, rw_s, sh_s = _trunk(x_sample, hg_s0, rw_s0, sh_s0, weights, C=CHUNK)

    unhg = lambda t: jnp.swapaxes(t, -1, -2)[None]
    unsh = lambda t: t.reshape(t.shape[0], 1, RW_BLK * LANES)[None]
    return (y_p, y_s,
            unhg(hg_p), _rw_unpack(rw_p)[None], unsh(sh_p),
            unhg(hg_s), _rw_unpack(rw_s)[None], unsh(sh_s))
```

```python
import functools

import jax
import jax.numpy as jnp
from jax import lax
from jax.experimental import pallas as pl
from jax.experimental.pallas import tpu as pltpu

F32 = jnp.float32
BF16 = jnp.bfloat16

LANES = 128
SUB = 8
D_MODEL = 2048
HG_HEADS = 8
HG_WIDTH = 1024
RW_PAIRS = 8
RW_HEAD = 64
RW_WIDTH = 1024
P_TOTAL = 8320
NBLK = P_TOTAL // LANES
HG_BLK = 32
RW_BLK = NBLK - HG_BLK
NORM_EPS = 1e-6
RW_GN_EPS = 64e-5
KK_EPS = 1e-12
VMEM_LIMIT = 56 * 1024 * 1024
LOG2E = 1.4426950408889634
CHUNK = 64
STREAMS_PER_STEP = 2
STAGGER = 1
CHUNKS_PER_STEP = 2
CHUNK_LAG = 6


def _bf(x):
    return x.astype(BF16)


def _dot(a, b):
    return jnp.dot(_bf(a), _bf(b), preferred_element_type=F32)


def _dot_nt(a, b):
    return lax.dot_general(_bf(a), _bf(b), (((1,), (1,)), ((), ())), preferred_element_type=F32)


def _dot_tn(a, b):
    return lax.dot_general(_bf(a), _bf(b), (((0,), (0,)), ((), ())), preferred_element_type=F32)


def _split3(x):
    hi = _bf(x)
    r = x - hi.astype(F32)
    mid = _bf(r)
    lo = _bf(r - mid.astype(F32))
    return hi, mid, lo


def _dot_exact_lhs(l_bf, x):
    return jnp.dot(jnp.concatenate([l_bf, l_bf, l_bf], axis=1), jnp.concatenate(_split3(x), axis=0),
                   preferred_element_type=F32)


def _group_sum2(xa, xb, ones2_bf):
    s = jnp.dot(_bf(jnp.concatenate([xa, xb], axis=1)), ones2_bf, preferred_element_type=F32)
    return s[:, :LANES], s[:, LANES:]


def _sigmoid(x):
    return 1.0 / (1.0 + jnp.exp2(x * (-LOG2E)))


def _inproj_kernel(x_ref, g_ref, w_ref, o_ref, u_ref, *, nb):
    @pl.when(pl.program_id(1) == 0)
    def _():
        x = x_ref[...]
        ms = jnp.mean(x * x, axis=-1, keepdims=True)
        u_ref[...] = _bf(x * lax.rsqrt(ms + NORM_EPS) * g_ref[...])

    acc = jnp.dot(u_ref[...], w_ref[...], preferred_element_type=F32)
    for cb in range(nb):
        o_ref[cb] = acc[:, cb * LANES:(cb + 1) * LANES]


def _inproj(x, g, w_bf, tm):
    rows = x.shape[0]
    nb = 13
    tn = nb * LANES
    return pl.pallas_call(
        functools.partial(_inproj_kernel, nb=nb),
        grid=(rows // tm, NBLK // nb),
        in_specs=[
            pl.BlockSpec((tm, D_MODEL), lambda i, j: (i, 0)),
            pl.BlockSpec((1, D_MODEL), lambda i, j: (0, 0)),
            pl.BlockSpec((D_MODEL, tn), lambda i, j: (0, j)),
        ],
        out_specs=pl.BlockSpec((nb, tm, LANES), lambda i, j: (j, i, 0)),
        out_shape=jax.ShapeDtypeStruct((NBLK, rows, LANES), F32),
        scratch_shapes=[pltpu.VMEM((tm, D_MODEL), BF16)],
        compiler_params=pltpu.CompilerParams(
            dimension_semantics=("arbitrary", "arbitrary"), vmem_limit_bytes=VMEM_LIMIT),
        name="inproj",
    )(x, g, w_bf)


def _mixer_kernel(proj_ref, hg0_ref, rw0_ref, sh0_ref, lb_ref, hgn_ref, mu_ref, w0_ref, a0_ref,
                  w2a_ref, kk_ref, ka_ref, rk_ref, lnw_ref, lnb_ref, wsel_ref,
                  out_ref, hgS_ref, rwS_ref, sh_ref,
                  q_all, k_all, g_all, lk_all, gk_all, lw_all, gam_all, cum_all, lora_all, *, C, NS, CPS, stagger, lag):
    @pl.when(pl.program_id(1) == 0)
    def _():
        hgS_ref[...] = hg0_ref[...]
        rwS_ref[...] = rw0_ref[...]
        sh_ref[...] = sh0_ref[...]

    row = lax.broadcasted_iota(jnp.int32, (C, LANES), 0)
    lane = lax.broadcasted_iota(jnp.int32, (C, LANES), 1)
    ri = lax.broadcasted_iota(jnp.int32, (C, C), 0)
    ci = lax.broadcasted_iota(jnp.int32, (C, C), 1)
    tril_bf = (ci <= ri).astype(F32).astype(BF16)
    levels = [b for b in (8, 16, 32) if 2 * b <= C]
    level_sign = {b: jnp.where((row & b) != 0, 1.0, -1.0) for b in levels}
    level_qk = {b: functools.partial(jnp.where, (row & b) != 0) for b in levels}
    level_keep = {b: ((ri & b) != 0) & ((ci & b) == 0) & ((ri & (-2 * b)) == (ci & (-2 * b))) for b in levels}
    diag_keep = ((lane >> 3) == (row >> 3)) & (lane <= row)
    head0 = lane < RW_HEAD
    blockdiag = ((lax.broadcasted_iota(jnp.int32, (LANES, LANES), 0) < RW_HEAD)
                 == (lax.broadcasted_iota(jnp.int32, (LANES, LANES), 1) < RW_HEAD))
    ones2_bf = ((lax.broadcasted_iota(jnp.int32, (2 * LANES, 2 * LANES), 0) >> 6)
                == (lax.broadcasted_iota(jnp.int32, (2 * LANES, 2 * LANES), 1) >> 6)).astype(F32).astype(BF16)
    C2 = 2 * C
    aligned = C2 % LANES == 0
    ri2 = lax.broadcasted_iota(jnp.int32, (C, C2), 0)
    ci2 = lax.broadcasted_iota(jnp.int32, (C, C2), 1)
    cj2 = jnp.where(ci2 >= C, ci2 - C, ci2)
    smask = cj2 < ri2
    imask = cj2 <= ri2
    first = ci2 < C
    eye2 = (lax.broadcasted_iota(jnp.int32, (C2, C2), 0)
            == lax.broadcasted_iota(jnp.int32, (C2, C2), 1)).astype(F32)
    n_dbl = C.bit_length() - 1
    half = HG_HEADS // 2
    pairs = range(RW_PAIRS)
    sls = [slice(p * LANES, (p + 1) * LANES) for p in pairs]

    h0 = head0.astype(F32).astype(BF16)
    h1 = (~head0).astype(F32).astype(BF16)

    def by_head(x):
        xb = _bf(x)
        return jnp.concatenate([xb * h0, xb * h1], axis=0)

    def two_rows(ref, base, hs):
        return jnp.concatenate([jnp.broadcast_to(ref[base:base + 1, hs], (SUB, LANES)),
                                jnp.broadcast_to(ref[base + SUB:base + SUB + 1, hs], (SUB, LANES))], axis=0)

    def program(s, cc):
        proj = proj_ref.at[:, s, pl.ds(cc * C, C)]
        out = out_ref.at[s, pl.ds(cc * C, C)]
        hgS, rwS, sh = hgS_ref.at[s], rwS_ref.at[s], sh_ref.at[s]
        w_ = s * CPS + cc
        q_s, k_s, g_s, lk_s, gk_s = q_all.at[w_], k_all.at[w_], g_all.at[w_], lk_all.at[w_], gk_all.at[w_]
        lw_s, gam_s, cum_s, lora_s = lw_all.at[w_], gam_all.at[w_], cum_all.at[w_], lora_all.at[w_]

        def hg_gates(_):
            for h in range(HG_HEADS):
                hs = slice(h * LANES, (h + 1) * LANES)
                pq = proj[h]
                lb = lb_ref[:, hs]
                forget = lb + (1.0 - lb) * _sigmoid(proj[HG_HEADS + h])
                k = 1.0 - forget
                q_s[:, hs] = pq * _sigmoid(pq)
                k_s[:, hs] = k
                lk_s[:, hs] = jnp.log(k) * LOG2E
                g_s[:, hs] = jnp.log(forget) * LOG2E
            cum_s[...] = _dot_exact_lhs(tril_bf, g_s[...])

        scores_off = [None] * HG_HEADS
        diag_lhs = [None] * HG_HEADS
        diags = {}

        def hg_scores(h):
            hs = slice(h * LANES, (h + 1) * LANES)
            q = q_s[:, hs]
            k = k_s[:, hs]
            lk = lk_s[:, hs]
            G = cum_s[0:C, hs]
            gk_s[:, hs] = G - lk
            scores = jnp.zeros((C, C), F32)
            for b in levels:
                gref = jnp.concatenate(
                    [jnp.broadcast_to(cum_s[2 * b * blk + b - 1:2 * b * blk + b, hs], (2 * b, LANES))
                     for blk in range(C // (2 * b))], axis=0)
                me = _bf(level_qk[b](q, k) * jnp.exp2((G - gref) * level_sign[b]))
                sc = lax.dot_general(me, me, (((1,), (1,)), ((), ())), preferred_element_type=F32)
                scores = jnp.where(level_keep[b], sc, scores)
            scores_off[h] = scores

            units = []
            for u in range(C // (2 * SUB)):
                r0 = u * 2 * SUB
                G16 = G[r0:r0 + 2 * SUB]
                q16 = _bf(q[r0:r0 + 2 * SUB])
                cols = []
                for j in range(SUB):
                    x = jnp.minimum(G16 - two_rows(gk_s, r0 + j, hs), two_rows(lk_s, r0 + j, hs))
                    cols.append(q16 * _bf(jnp.exp2(x)))
                units.append(jnp.concatenate(cols, axis=1))
            diag_lhs[h] = jnp.concatenate(units, axis=0)

        def hg_diag(part):
            lhs = jnp.concatenate(diag_lhs[part * half:(part + 1) * half], axis=0)
            diags[part] = jnp.dot(lhs, wsel_ref[...], preferred_element_type=F32)

        def hg_finish(h):
            hs = slice(h * LANES, (h + 1) * LANES)
            diag_h = diags[h // half][(h % half) * C:(h % half + 1) * C]
            G = cum_s[0:C, hs]
            v = proj[2 * HG_HEADS + h]
            pg = proj[3 * HG_HEADS + h]
            St = hgS[h]
            scores = scores_off[h] + jnp.where(diag_keep, diag_h, 0.0)[:, :C]
            o = _dot_nt(q_s[:, hs] * jnp.exp2(G), St) + _dot(scores, v)

            Gl = G[C - 1:C, :]
            hgS[h] = St * jnp.exp2(Gl) + _dot_tn(v, jnp.exp2(Gl - gk_s[:, hs]))

            o = o * lax.rsqrt(jnp.mean(o * o, axis=-1, keepdims=True) + NORM_EPS)
            o = o * hgn_ref[:, hs] * (pg * _sigmoid(pg))
            out[:, hs] = o.astype(out.dtype)

        S_, F_, D_ = hg_scores, hg_finish, hg_diag
        fillers = [(hg_gates, 0), (S_, 0), (S_, 1), (S_, 2), (S_, 3), (D_, 0), (S_, 4), (S_, 5), (F_, 0), (F_, 1),
                   (S_, 6), (F_, 2), (F_, 3), (S_, 7), (D_, 1), (F_, 4), (F_, 5), (F_, 6), (F_, 7)]

        def fill(n):
            for _ in range(min(n, len(fillers))):
                fn, arg = fillers.pop(0)
                fn(arg)

        def shifted(blk):
            p = proj[HG_BLK + blk]
            prev = pltpu.roll(p, 1, axis=0)
            carried = sh[blk] if cc == 0 else proj_ref[HG_BLK + blk, s, cc * C - 1:cc * C, :]
            prev = jnp.concatenate([jnp.where(row[:SUB] == 0, carried, prev[:SUB]), prev[SUB:]], axis=0)
            return p + (prev - p) * mu_ref[blk]

        xl = shifted(RW_BLK - 1)
        xl = jnp.where(lane < RW_HEAD, jnp.tanh(xl), xl)
        lora_s[...] = jnp.dot(_bf(xl), w2a_ref[...], preferred_element_type=F32)

        st = [dict() for _ in pairs]
        duos = [(st[i], st[i + 1]) for i in range(0, RW_PAIRS, 2)]

        for p, d in zip(pairs, st):
            sl = sls[p]
            d["r"] = shifted(p)
            d["xk"] = shifted(RW_PAIRS + p)
            d["v"] = shifted(2 * RW_PAIRS + p)
            w = w0_ref[:, sl] + lora_s[:, sl]
            lw_s[:, sl] = (-0.6065306597126334 * LOG2E) * _sigmoid(w)
            d["a"] = _sigmoid(a0_ref[:, sl] + lora_s[:, RW_WIDTH + p * LANES:RW_WIDTH + (p + 1) * LANES])
            d["kkr"] = d["xk"] * kk_ref[:, sl]
            d["k"] = d["xk"] * (1.0 + (d["a"] - 1.0) * ka_ref[:, sl])
        yield
        for da, db in duos:
            for d, ss in zip((da, db), _group_sum2(da["kkr"] * da["kkr"], db["kkr"] * db["kkr"], ones2_bf)):
                d["kk"] = d["kkr"] / jnp.maximum(jnp.sqrt(ss), KK_EPS)
                d["b"] = d["kk"] * d["a"]
        for p, (da, db) in enumerate(duos):
            da["bonus"], db["bonus"] = _group_sum2(da["r"] * da["k"] * rk_ref[:, sls[2 * p]],
                                                   db["r"] * db["k"] * rk_ref[:, sls[2 * p + 1]], ones2_bf)

        gam_s[...] = _dot_exact_lhs(tril_bf, lw_s[...])
        yield

        for p, d in zip(pairs, st):
            sl = sls[p]
            gam = gam_s[:, sl]
            eg = jnp.exp2(gam)
            eng = jnp.exp2(-gam)
            at = -d["kk"] * jnp.exp2(gam - lw_s[:, sl])
            rt = d["r"] * eg
            bt = d["b"] * eng
            kt = d["k"] * eng
            dl = eg[C - 1:C, :] * eng
            d["bk_dl"] = _bf(jnp.concatenate([d["b"] * dl, d["k"] * dl], axis=0))
            d["eg_last"] = eg[C - 1:C, :]
            at_rt = d["at_rt"] = _bf(jnp.concatenate([at, rt], axis=0))
            bt, kt = _bf(bt), _bf(kt)
            rhs0 = jnp.concatenate([bt * h0, kt * h0], axis=0)
            rhs1 = jnp.concatenate([kt * h1, bt * h1], axis=0)
            if aligned:
                m = _dot_nt(at_rt, jnp.concatenate([rhs0, rhs1], axis=0))
                m0, m1 = m[:, :C2], m[:, C2:]
            else:
                m0, m1 = _dot_nt(at_rt, rhs0), _dot_nt(at_rt, rhs1)
            aa0 = jnp.where(smask, m0[:C], 0.0)
            aa1 = jnp.where(smask, m1[:C], 0.0)
            d["ak"] = _bf(jnp.where(first, aa1, aa0))
            d["ar0"] = _bf(jnp.where(imask, m0[C:], 0.0))
            d["ar1"] = _bf(jnp.where(imask, m1[C:], 0.0))
            pw = jnp.concatenate([jnp.where(first, aa0, 0.0), jnp.where(first, 0.0, aa1)], axis=0)
            d["t"] = eye2 + pw
            d["pw"] = _bf(pw)
        fill(1)
        yield

        for d in st:
            d["pw"] = _bf(_dot(d["pw"], d["pw"]))
        fill(1)
        yield
        for _ in range(n_dbl - 2):
            for d in st:
                if aligned:
                    res = _dot(d["pw"], jnp.concatenate([d["pw"], _bf(d["t"])], axis=1))
                    d["pw"], d["t"] = _bf(res[:, :C2]), d["t"] + res[:, C2:]
                else:
                    d["pw"], d["t"] = _bf(_dot(d["pw"], d["pw"])), d["t"] + _dot(d["pw"], d["t"])
            fill(1)
            yield

        for p, d in zip(pairs, st):
            d["S"] = rwS[p]
            d["wst"] = _dot_nt(d["at_rt"], d["S"])
            vb = _bf(d["v"])
            d["v0"], d["v1"] = vb * h0, vb * h1
            akv = _dot(d["ak"], jnp.concatenate([d["v1"], d["v0"]], axis=0))
            d["rhs2"] = by_head(d["wst"][:C] + akv)
        fill(1)
        yield
        for d in st:
            d["w"] = _dot(d["t"][:C] + d["t"][C:], d["rhs2"])
        fill(3)
        yield
        for d in st:
            d["u"] = d["w"] + _dot(d["pw"][:C] + d["pw"][C:], by_head(d["w"]))
        fill(3)
        yield

        for p, d in zip(pairs, st):
            u, v = d["u"], d["v"]
            ub = _bf(u)
            u0, u1 = ub * h0, ub * h1
            if aligned:
                yuv = _dot(jnp.concatenate([d["ar0"], d["ar1"]], axis=1),
                           jnp.concatenate([u0, d["v0"], d["v1"], u1], axis=0))
            else:
                yuv = (_dot(d["ar0"], jnp.concatenate([u0, d["v0"]], axis=0))
                       + _dot(d["ar1"], jnp.concatenate([d["v1"], u1], axis=0)))
            d["y"] = d["wst"][C:] + yuv
            upd = _dot_tn(jnp.concatenate([ub, _bf(v)], axis=0), d["bk_dl"])
            rwS[p] = d["S"] * d["eg_last"] + jnp.where(blockdiag, upd, 0.0)
        fill(1)
        yield

        for da, db in duos:
            ma, mb = _group_sum2(da["y"], db["y"], ones2_bf)
            da["yc"] = da["y"] - ma * (1.0 / RW_HEAD)
            db["yc"] = db["y"] - mb * (1.0 / RW_HEAD)
        fill(1)
        yield
        for da, db in duos:
            da["var"], db["var"] = _group_sum2(da["yc"] * da["yc"], db["yc"] * db["yc"], ones2_bf)
        fill(2)
        yield
        for p, d in zip(pairs, st):
            sl = sls[p]
            yn = d["yc"] * lax.rsqrt(d["var"] * (1.0 / RW_HEAD) + RW_GN_EPS) * lnw_ref[:, sl] + lnb_ref[:, sl]
            xg = shifted(3 * RW_PAIRS + p)
            yo = (yn + d["bonus"] * d["v"]) * (xg * _sigmoid(xg))
            out[:, HG_WIDTH + p * LANES:HG_WIDTH + (p + 1) * LANES] = yo.astype(out.dtype)
        fill(len(fillers))

        if cc == CPS - 1:
            for blk in range(RW_BLK):
                sh[blk] = proj[HG_BLK + blk, C - 1:C, :]

    todo = [(cc * lag + s * stagger, program(s, cc)) for cc in range(CPS) for s in range(NS)]
    rnd = 0
    while todo:
        todo = [(t0, g) for t0, g in todo if rnd < t0 or next(g, StopIteration) is not StopIteration]
        rnd += 1


def _mixer(proj, hg0, rw0, sh0, params, B, T, C):
    nc = T // C
    NS = STREAMS_PER_STEP if B % STREAMS_PER_STEP == 0 else 1
    CPS = CHUNKS_PER_STEP if nc % CHUNKS_PER_STEP == 0 else 1
    full = lambda shape: pl.BlockSpec(shape, lambda b, c: (0,) * len(shape))
    per_b = lambda shape: pl.BlockSpec((NS,) + shape, lambda b, c: (b,) + (0,) * len(shape))
    lb, hgn, mu, w0, a0, w2a, kk, ka, rk, lnw, lnb = params
    wsel = ((jnp.arange(SUB * LANES)[:, None] // LANES) == (jnp.arange(LANES)[None, :] % SUB)).astype(BF16)
    in_specs = [
        pl.BlockSpec((NBLK, NS, CPS * C, LANES), lambda b, c: (0, b, c, 0)),
        per_b((HG_HEADS, LANES, LANES)),
        per_b((RW_PAIRS, LANES, LANES)),
        per_b((RW_BLK, 1, LANES)),
        full((1, HG_WIDTH)), full((1, HG_WIDTH)), full((RW_BLK, 1, LANES)),
        full((1, RW_WIDTH)), full((1, RW_WIDTH)), full((LANES, 2 * RW_WIDTH)),
        full((1, RW_WIDTH)), full((1, RW_WIDTH)), full((1, RW_WIDTH)),
        full((1, RW_WIDTH)), full((1, RW_WIDTH)), full((SUB * LANES, LANES)),
    ]
    out_specs = [
        pl.BlockSpec((NS, CPS * C, D_MODEL), lambda b, c: (b, c, 0)),
        per_b((HG_HEADS, LANES, LANES)),
        per_b((RW_PAIRS, LANES, LANES)),
        per_b((RW_BLK, 1, LANES)),
    ]
    out_shape = [
        jax.ShapeDtypeStruct((B, T, D_MODEL), BF16),
        jax.ShapeDtypeStruct((B, HG_HEADS, LANES, LANES), F32),
        jax.ShapeDtypeStruct((B, RW_PAIRS, LANES, LANES), F32),
        jax.ShapeDtypeStruct((B, RW_BLK, 1, LANES), F32),
    ]
    scratch = ([pltpu.VMEM((NS * CPS, C, HG_WIDTH), F32)] * 8
               + [pltpu.VMEM((NS * CPS, C, 2 * RW_WIDTH), F32)])
    mixed, hg, rw, sh = pl.pallas_call(
        functools.partial(_mixer_kernel, C=C, NS=NS, CPS=CPS, stagger=STAGGER, lag=CHUNK_LAG),
        grid=(B // NS, nc // CPS),
        in_specs=in_specs,
        out_specs=out_specs,
        out_shape=out_shape,
        scratch_shapes=scratch,
        compiler_params=pltpu.CompilerParams(
            dimension_semantics=("arbitrary", "arbitrary"), vmem_limit_bytes=VMEM_LIMIT),
        name="mixer",
    )(proj.reshape(NBLK, B, T, LANES), hg0, rw0, sh0, lb, hgn, mu, w0, a0, w2a, kk, ka, rk, lnw, lnb, wsel)
    return mixed.reshape(B * T, D_MODEL), hg, rw, sh


def _outproj_kernel(m_ref, w_ref, g_ref, x_ref, o_ref):
    z = jnp.dot(m_ref[...], w_ref[...], preferred_element_type=F32)
    ms = jnp.mean(z * z, axis=-1, keepdims=True)
    o_ref[...] = x_ref[...] + z * lax.rsqrt(ms + NORM_EPS) * g_ref[...]


def _outproj(mixed, w_bf, g, x, tm):
    rows = x.shape[0]
    return pl.pallas_call(
        _outproj_kernel,
        grid=(rows // tm,),
        in_specs=[
            pl.BlockSpec((tm, D_MODEL), lambda i: (i, 0)),
            pl.BlockSpec((D_MODEL, D_MODEL), lambda i: (0, 0)),
            pl.BlockSpec((1, D_MODEL), lambda i: (0, 0)),
            pl.BlockSpec((tm, D_MODEL), lambda i: (i, 0)),
        ],
        out_specs=pl.BlockSpec((tm, D_MODEL), lambda i: (i, 0)),
        out_shape=jax.ShapeDtypeStruct((rows, D_MODEL), F32),
        compiler_params=pltpu.CompilerParams(
            dimension_semantics=("arbitrary",), vmem_limit_bytes=VMEM_LIMIT),
        name="outproj",
    )(mixed, w_bf, g, x)


def _rw_pack(s):
    B = s.shape[0]
    s = s.reshape(B, RW_PAIRS, 2, RW_HEAD, RW_HEAD)
    z = jnp.zeros_like(s[:, :, 0])
    top = jnp.concatenate([s[:, :, 0], z], axis=-1)
    bot = jnp.concatenate([z, s[:, :, 1]], axis=-1)
    return jnp.concatenate([top, bot], axis=-2)


def _rw_unpack(s):
    B = s.shape[0]
    a = s[:, :, :RW_HEAD, :RW_HEAD]
    b = s[:, :, RW_HEAD:, RW_HEAD:]
    return jnp.stack([a, b], axis=2).reshape(B, 2 * RW_PAIRS, RW_HEAD, RW_HEAD)


def _pick_tile(rows, cap):
    t = cap
    while rows % t:
        t //= 2
    return t


def _trunk(x, hg0, rw0, sh0, weights, C, need_y=True):
    (norm_pre, w_in_bf, mixer_params, w_out_bf, norm_post) = weights
    B, T, _ = x.shape
    x2 = x.reshape(B * T, D_MODEL)
    proj = _inproj(x2, norm_pre, w_in_bf, _pick_tile(B * T, 1024))
    mixed, hg, rw, sh = _mixer(proj, hg0, rw0, sh0, mixer_params, B, T, C)
    y = None
    if need_y:
        y = _outproj(mixed, w_out_bf, norm_post, x2, _pick_tile(B * T, 512)).reshape(B, T, D_MODEL)
    return y, hg, rw, sh


def kernel(x_prompt, x_sample, state_hgrn, state_rwkv, state_shift, meta_tokens, norm_pre, w_in,
           hg_lower_bounds, hg_norm, rw_mu, rw_w0, rw_w2, rw_a0, rw_a2, rw_k_k, rw_k_a, rw_r_k,
           rw_ln_w, rw_ln_b, w_out, norm_post):
    B = x_prompt.shape[0]
    Bs = x_sample.shape[0]
    n_meta = meta_tokens.shape[0]
    l = 0

    lbs = jnp.cumsum(jax.nn.softmax(hg_lower_bounds.astype(F32), axis=0), axis=0)[l].reshape(1, HG_WIDTH)
    z64 = jnp.zeros((RW_HEAD, RW_WIDTH), F32)
    w2a = jnp.concatenate([jnp.concatenate([rw_w2[l], z64], axis=1),
                           jnp.concatenate([z64, rw_a2[l]], axis=1)], axis=0).astype(BF16)
    row = lambda t: t.reshape(1, -1).astype(F32)
    mixer_params = (lbs, row(hg_norm[l]), rw_mu[l].reshape(RW_BLK, 1, LANES), row(rw_w0[l]), row(rw_a0[l]),
                    w2a, row(rw_k_k[l]), row(rw_k_a[l]), row(rw_r_k[l]), row(rw_ln_w[l]), row(rw_ln_b[l]))
    weights = (row(norm_pre[l]), w_in[l].astype(BF16), mixer_params, w_out[l].astype(BF16), row(norm_post[l]))

    zero_hg = jnp.zeros((1, HG_HEADS, LANES, LANES), F32)
    zero_rw = jnp.zeros((1, RW_PAIRS, LANES, LANES), F32)
    zero_sh = jnp.zeros((1, RW_BLK, 1, LANES), F32)
    _, hg_m, rw_m, sh_m = _trunk(meta_tokens[None].astype(F32), zero_hg, zero_rw, zero_sh, weights,
                                 C=n_meta, need_y=False)

    rep = lambda t: jnp.broadcast_to(t, (B,) + t.shape[1:])
    y_p, hg_p, rw_p, sh_p = _trunk(x_prompt, rep(hg_m), rep(rw_m), rep(sh_m), weights, C=CHUNK)

    hg_s0 = jnp.swapaxes(state_hgrn[l], -1, -2)
    rw_s0 = _rw_pack(state_rwkv[l])
    sh_s0 = state_shift[l].reshape(Bs, RW_BLK, 1, LANES)
    y_s, hg_s, rw_s, sh_s = _trunk(x_sample, hg_s0, rw_s0, sh_s0, weights, C=CHUNK)

    unhg = lambda t: jnp.swapaxes(t, -1, -2)[None]
    unsh = lambda t: t.reshape(t.shape[0], 1, RW_BLK * LANES)[None]
    return (y_p, y_s,
            unhg(hg_p), _rw_unpack(rw_p)[None], unsh(sh_p),
            unhg(hg_s), _rw_unpack(rw_s)[None], unsh(sh_s))
```

```python
import functools

import jax
import jax.numpy as jnp
from jax import lax
from jax.experimental import pallas as pl
from jax.experimental.pallas import tpu as pltpu

F32 = jnp.float32
BF16 = jnp.bfloat16

LANES = 128
SUB = 8
D_MODEL = 2048
HG_HEADS = 8
HG_WIDTH = 1024
RW_PAIRS = 8
RW_HEAD = 64
RW_WIDTH = 1024
P_TOTAL = 8320
NBLK = P_TOTAL // LANES
HG_BLK = 32
RW_BLK = NBLK - HG_BLK
NORM_EPS = 1e-6
RW_GN_EPS = 64e-5
KK_EPS = 1e-12
VMEM_LIMIT = 56 * 1024 * 1024
LOG2E = 1.4426950408889634
CHUNK = 64
STREAMS_PER_STEP = 2
STAGGER = 1


def _bf(x):
    return x.astype(BF16)


def _dot(a, b):
    return jnp.dot(_bf(a), _bf(b), preferred_element_type=F32)


def _dot_nt(a, b):
    return lax.dot_general(_bf(a), _bf(b), (((1,), (1,)), ((), ())), preferred_element_type=F32)


def _dot_tn(a, b):
    return lax.dot_general(_bf(a), _bf(b), (((0,), (0,)), ((), ())), preferred_element_type=F32)


def _dot_exact_lhs(l_bf, x):
    hi = _bf(x)
    lo = _bf(x - hi.astype(F32))
    return jnp.dot(jnp.concatenate([l_bf, l_bf], axis=1), jnp.concatenate([hi, lo], axis=0),
                   preferred_element_type=F32)


def _sigmoid(x):
    return 1.0 / (1.0 + jnp.exp2(x * (-LOG2E)))


def _inproj_kernel(x_ref, g_ref, w_ref, o_ref, u_ref, *, nb):
    @pl.when(pl.program_id(1) == 0)
    def _():
        x = x_ref[...]
        ms = jnp.mean(x * x, axis=-1, keepdims=True)
        u_ref[...] = _bf(x * lax.rsqrt(ms + NORM_EPS) * g_ref[...])

    acc = jnp.dot(u_ref[...], w_ref[...], preferred_element_type=F32)
    for cb in range(nb):
        o_ref[cb] = acc[:, cb * LANES:(cb + 1) * LANES]


def _inproj(x, g, w_bf, tm):
    rows = x.shape[0]
    nb = 13
    tn = nb * LANES
    return pl.pallas_call(
        functools.partial(_inproj_kernel, nb=nb),
        grid=(rows // tm, NBLK // nb),
        in_specs=[
            pl.BlockSpec((tm, D_MODEL), lambda i, j: (i, 0)),
            pl.BlockSpec((1, D_MODEL), lambda i, j: (0, 0)),
            pl.BlockSpec((D_MODEL, tn), lambda i, j: (0, j)),
        ],
        out_specs=pl.BlockSpec((nb, tm, LANES), lambda i, j: (j, i, 0)),
        out_shape=jax.ShapeDtypeStruct((NBLK, rows, LANES), F32),
        scratch_shapes=[pltpu.VMEM((tm, D_MODEL), BF16)],
        compiler_params=pltpu.CompilerParams(
            dimension_semantics=("arbitrary", "arbitrary"), vmem_limit_bytes=VMEM_LIMIT),
        name="inproj",
    )(x, g, w_bf)


def _mixer_kernel(proj_ref, hg0_ref, rw0_ref, sh0_ref, lb_ref, hgn_ref, mu_ref, w0_ref, a0_ref,
                  w2a_ref, kk_ref, ka_ref, rk_ref, lnw_ref, lnb_ref, wsel_ref,
                  out_ref, hgS_ref, rwS_ref, sh_ref,
                  q_all, k_all, g_all, lk_all, gk_all, lw_all, gam_all, cum_all, lora_all, *, C, NS, stagger):
    @pl.when(pl.program_id(1) == 0)
    def _():
        hgS_ref[...] = hg0_ref[...]
        rwS_ref[...] = rw0_ref[...]
        sh_ref[...] = sh0_ref[...]

    row = lax.broadcasted_iota(jnp.int32, (C, LANES), 0)
    lane = lax.broadcasted_iota(jnp.int32, (C, LANES), 1)
    ri = lax.broadcasted_iota(jnp.int32, (C, C), 0)
    ci = lax.broadcasted_iota(jnp.int32, (C, C), 1)
    tril_bf = (ci <= ri).astype(F32).astype(BF16)
    levels = [b for b in (8, 16, 32) if 2 * b <= C]
    level_sign = {b: jnp.where((row & b) != 0, 1.0, -1.0) for b in levels}
    level_qk = {b: functools.partial(jnp.where, (row & b) != 0) for b in levels}
    level_keep = {b: ((ri & b) != 0) & ((ci & b) == 0) & ((ri & (-2 * b)) == (ci & (-2 * b))) for b in levels}
    diag_keep = ((lane >> 3) == (row >> 3)) & (lane <= row)
    head0 = lane < RW_HEAD
    blockdiag = ((lax.broadcasted_iota(jnp.int32, (LANES, LANES), 0) < RW_HEAD)
                 == (lax.broadcasted_iota(jnp.int32, (LANES, LANES), 1) < RW_HEAD))
    ones_bf = blockdiag.astype(F32).astype(BF16)

    def group_sums(xs):
        s = jnp.dot(_bf(jnp.concatenate(xs, axis=0)), ones_bf, preferred_element_type=F32)
        return [s[i * C:(i + 1) * C] for i in range(len(xs))]
    C2 = 2 * C
    aligned = C2 % LANES == 0
    ri2 = lax.broadcasted_iota(jnp.int32, (C, C2), 0)
    ci2 = lax.broadcasted_iota(jnp.int32, (C, C2), 1)
    cj2 = jnp.where(ci2 >= C, ci2 - C, ci2)
    smask = cj2 < ri2
    imask = cj2 <= ri2
    first = ci2 < C
    eye2 = (lax.broadcasted_iota(jnp.int32, (C2, C2), 0)
            == lax.broadcasted_iota(jnp.int32, (C2, C2), 1)).astype(F32)
    n_dbl = C.bit_length() - 1
    half = HG_HEADS // 2
    pairs = range(RW_PAIRS)
    sls = [slice(p * LANES, (p + 1) * LANES) for p in pairs]

    h0 = head0.astype(F32).astype(BF16)
    h1 = (~head0).astype(F32).astype(BF16)

    def by_head(x):
        xb = _bf(x)
        return jnp.concatenate([xb * h0, xb * h1], axis=0)

    def two_rows(ref, base, hs):
        return jnp.concatenate([jnp.broadcast_to(ref[base:base + 1, hs], (SUB, LANES)),
                                jnp.broadcast_to(ref[base + SUB:base + SUB + 1, hs], (SUB, LANES))], axis=0)

    def program(s):
        proj = proj_ref.at[:, s]
        out = out_ref.at[s]
        hgS, rwS, sh = hgS_ref.at[s], rwS_ref.at[s], sh_ref.at[s]
        q_s, k_s, g_s, lk_s, gk_s = q_all.at[s], k_all.at[s], g_all.at[s], lk_all.at[s], gk_all.at[s]
        lw_s, gam_s, cum_s, lora_s = lw_all.at[s], gam_all.at[s], cum_all.at[s], lora_all.at[s]

        def hg_gates(_):
            for h in range(HG_HEADS):
                hs = slice(h * LANES, (h + 1) * LANES)
                pq = proj[h]
                lb = lb_ref[:, hs]
                forget = lb + (1.0 - lb) * _sigmoid(proj[HG_HEADS + h])
                k = 1.0 - forget
                q_s[:, hs] = pq * _sigmoid(pq)
                k_s[:, hs] = k
                lk_s[:, hs] = jnp.log2(k)
                g_s[:, hs] = jnp.log2(forget)
            cum_s[...] = _dot_exact_lhs(tril_bf, g_s[...])

        scores_off = [None] * HG_HEADS
        diag_lhs = [None] * HG_HEADS
        diags = {}

        def hg_scores(h):
            hs = slice(h * LANES, (h + 1) * LANES)
            q = q_s[:, hs]
            k = k_s[:, hs]
            lk = lk_s[:, hs]
            G = cum_s[0:C, hs]
            gk_s[:, hs] = G - lk
            scores = jnp.zeros((C, C), F32)
            for b in levels:
                gref = jnp.concatenate(
                    [jnp.broadcast_to(cum_s[2 * b * blk + b - 1:2 * b * blk + b, hs], (2 * b, LANES))
                     for blk in range(C // (2 * b))], axis=0)
                me = _bf(level_qk[b](q, k) * jnp.exp2((G - gref) * level_sign[b]))
                sc = lax.dot_general(me, me, (((1,), (1,)), ((), ())), preferred_element_type=F32)
                scores = jnp.where(level_keep[b], sc, scores)
            scores_off[h] = scores

            units = []
            for u in range(C // (2 * SUB)):
                r0 = u * 2 * SUB
                G16 = G[r0:r0 + 2 * SUB]
                q16 = _bf(q[r0:r0 + 2 * SUB])
                cols = []
                for j in range(SUB):
                    x = jnp.minimum(G16 - two_rows(gk_s, r0 + j, hs), two_rows(lk_s, r0 + j, hs))
                    cols.append(q16 * _bf(jnp.exp2(x)))
                units.append(jnp.concatenate(cols, axis=1))
            diag_lhs[h] = jnp.concatenate(units, axis=0)

        def hg_diag(part):
            lhs = jnp.concatenate(diag_lhs[part * half:(part + 1) * half], axis=0)
            diags[part] = jnp.dot(lhs, wsel_ref[...], preferred_element_type=F32)

        def hg_finish(h):
            hs = slice(h * LANES, (h + 1) * LANES)
            diag_h = diags[h // half][(h % half) * C:(h % half + 1) * C]
            G = cum_s[0:C, hs]
            v = proj[2 * HG_HEADS + h]
            pg = proj[3 * HG_HEADS + h]
            St = hgS[h]
            scores = scores_off[h] + jnp.where(diag_keep, diag_h, 0.0)[:, :C]
            o = _dot_nt(q_s[:, hs] * jnp.exp2(G), St) + _dot(scores, v)

            Gl = G[C - 1:C, :]
            hgS[h] = St * jnp.exp2(Gl) + _dot_tn(v, jnp.exp2(Gl - gk_s[:, hs]))

            o = o * lax.rsqrt(jnp.mean(o * o, axis=-1, keepdims=True) + NORM_EPS)
            o = o * hgn_ref[:, hs] * (pg * _sigmoid(pg))
            out[:, hs] = o.astype(out.dtype)

        S_, F_, D_ = hg_scores, hg_finish, hg_diag
        fillers = [(hg_gates, 0), (S_, 0), (S_, 1), (S_, 2), (S_, 3), (D_, 0), (S_, 4), (S_, 5), (F_, 0), (F_, 1),
                   (S_, 6), (F_, 2), (F_, 3), (S_, 7), (D_, 1), (F_, 4), (F_, 5), (F_, 6), (F_, 7)]

        def fill(n):
            for _ in range(min(n, len(fillers))):
                fn, arg = fillers.pop(0)
                fn(arg)

        def shifted(blk):
            p = proj[HG_BLK + blk]
            prev = pltpu.roll(p, 1, axis=0)
            prev = jnp.concatenate([jnp.where(row[:SUB] == 0, sh[blk], prev[:SUB]), prev[SUB:]], axis=0)
            return p + (prev - p) * mu_ref[blk]

        xl = shifted(RW_BLK - 1)
        xl = jnp.where(lane < RW_HEAD, jnp.tanh(xl), xl)
        lora_s[...] = jnp.dot(_bf(xl), w2a_ref[...], preferred_element_type=F32)

        st = [dict() for _ in pairs]

        for p, d in zip(pairs, st):
            sl = sls[p]
            d["r"] = shifted(p)
            d["xk"] = shifted(RW_PAIRS + p)
            d["v"] = shifted(2 * RW_PAIRS + p)
            w = w0_ref[:, sl] + lora_s[:, sl]
            lw_s[:, sl] = (-0.6065306597126334 * LOG2E) * _sigmoid(w)
            d["a"] = _sigmoid(a0_ref[:, sl] + lora_s[:, RW_WIDTH + p * LANES:RW_WIDTH + (p + 1) * LANES])
            d["kkr"] = d["xk"] * kk_ref[:, sl]
            d["k"] = d["xk"] * (1.0 + (d["a"] - 1.0) * ka_ref[:, sl])
        yield
        for d, ss in zip(st, group_sums([d["kkr"] * d["kkr"] for d in st])):
            d["kk"] = d["kkr"] * lax.rsqrt(jnp.maximum(ss, KK_EPS * KK_EPS))
            d["b"] = d["kk"] * d["a"]
        for d, bonus in zip(st, group_sums([d["r"] * d["k"] * rk_ref[:, sls[p]] for p, d in zip(pairs, st)])):
            d["bonus"] = bonus

        gam_s[...] = _dot_exact_lhs(tril_bf, lw_s[...])
        yield

        for p, d in zip(pairs, st):
            sl = sls[p]
            gam = gam_s[:, sl]
            eg = jnp.exp2(gam)
            eng = 1.0 / eg
            at = -d["kk"] * jnp.exp2(gam - lw_s[:, sl])
            rt = d["r"] * eg
            bt = d["b"] * eng
            kt = d["k"] * eng
            dl = eg[C - 1:C, :] * eng
            d["bk_dl"] = _bf(jnp.concatenate([d["b"] * dl, d["k"] * dl], axis=0))
            d["eg_last"] = eg[C - 1:C, :]
            d["S"] = rwS[p]
            at_rt = _bf(jnp.concatenate([at, rt], axis=0))
            bt, kt = _bf(bt), _bf(kt)
            rhs0 = jnp.concatenate([bt * h0, kt * h0], axis=0)
            rhs1 = jnp.concatenate([kt * h1, bt * h1], axis=0)
            if aligned:
                m = _dot_nt(at_rt, jnp.concatenate([rhs0, rhs1, _bf(d["S"])], axis=0))
                m0, m1, d["wst"] = m[:, :C2], m[:, C2:2 * C2], m[:, 2 * C2:]
            else:
                m0, m1, d["wst"] = _dot_nt(at_rt, rhs0), _dot_nt(at_rt, rhs1), _dot_nt(at_rt, d["S"])
            aa0 = jnp.where(smask, m0[:C], 0.0)
            aa1 = jnp.where(smask, m1[:C], 0.0)
            d["ak"] = _bf(jnp.where(first, aa1, aa0))
            d["ar0"] = _bf(jnp.where(imask, m0[C:], 0.0))
            d["ar1"] = _bf(jnp.where(imask, m1[C:], 0.0))
            pw = jnp.concatenate([jnp.where(first, aa0, 0.0), jnp.where(first, 0.0, aa1)], axis=0)
            d["t"] = eye2 + pw
            d["pw"] = _bf(pw)
        fill(1)
        yield

        for d in st:
            d["pw"] = _bf(_dot(d["pw"], d["pw"]))
        fill(1)
        yield
        for _ in range(n_dbl - 2):
            for d in st:
                if aligned:
                    res = _dot(d["pw"], jnp.concatenate([d["pw"], _bf(d["t"])], axis=1))
                    d["pw"], d["t"] = _bf(res[:, :C2]), d["t"] + res[:, C2:]
                else:
                    d["pw"], d["t"] = _bf(_dot(d["pw"], d["pw"])), d["t"] + _dot(d["pw"], d["t"])
            fill(1)
            yield

        for d in st:
            vb = _bf(d["v"])
            d["v0"], d["v1"] = vb * h0, vb * h1
            akv = _dot(d["ak"], jnp.concatenate([d["v1"], d["v0"]], axis=0))
            d["rhs2"] = by_head(d["wst"][:C] + akv)
        fill(1)
        yield
        for d in st:
            d["w"] = _dot(d["t"][:C] + d["t"][C:], d["rhs2"])
        fill(3)
        yield
        for d in st:
            d["u"] = d["w"] + _dot(d["pw"][:C] + d["pw"][C:], by_head(d["w"]))
        fill(3)
        yield

        for p, d in zip(pairs, st):
            u, v = d["u"], d["v"]
            ub = _bf(u)
            u0, u1 = ub * h0, ub * h1
            if aligned:
                yuv = _dot(jnp.concatenate([d["ar0"], d["ar1"]], axis=1),
                           jnp.concatenate([u0, d["v0"], d["v1"], u1], axis=0))
            else:
                yuv = (_dot(d["ar0"], jnp.concatenate([u0, d["v0"]], axis=0))
                       + _dot(d["ar1"], jnp.concatenate([d["v1"], u1], axis=0)))
            d["y"] = d["wst"][C:] + yuv
            upd = _dot_tn(jnp.concatenate([ub, _bf(v)], axis=0), d["bk_dl"])
            rwS[p] = d["S"] * d["eg_last"] + jnp.where(blockdiag, upd, 0.0)
        fill(1)
        yield

        for d, m in zip(st, group_sums([d["y"] for d in st])):
            d["yc"] = d["y"] - m * (1.0 / RW_HEAD)
        fill(1)
        yield
        for d, var in zip(st, group_sums([d["yc"] * d["yc"] for d in st])):
            d["var"] = var
        fill(2)
        yield
        for p, d in zip(pairs, st):
            sl = sls[p]
            yn = d["yc"] * lax.rsqrt(d["var"] * (1.0 / RW_HEAD) + RW_GN_EPS) * lnw_ref[:, sl] + lnb_ref[:, sl]
            xg = shifted(3 * RW_PAIRS + p)
            yo = (yn + d["bonus"] * d["v"]) * (xg * _sigmoid(xg))
            out[:, HG_WIDTH + p * LANES:HG_WIDTH + (p + 1) * LANES] = yo.astype(out.dtype)
        fill(len(fillers))

        for blk in range(RW_BLK):
            sh[blk] = proj[HG_BLK + blk, C - 1:C, :]

    todo = [(s * stagger, program(s)) for s in range(NS)]
    rnd = 0
    while todo:
        todo = [(t0, g) for t0, g in todo if rnd < t0 or next(g, StopIteration) is not StopIteration]
        rnd += 1


def _mixer(proj, hg0, rw0, sh0, params, B, T, C):
    nc = T // C
    NS = STREAMS_PER_STEP if B % STREAMS_PER_STEP == 0 else 1
    full = lambda shape: pl.BlockSpec(shape, lambda b, c: (0,) * len(shape))
    per_b = lambda shape: pl.BlockSpec((NS,) + shape, lambda b, c: (b,) + (0,) * len(shape))
    lb, hgn, mu, w0, a0, w2a, kk, ka, rk, lnw, lnb = params
    wsel = ((jnp.arange(SUB * LANES)[:, None] // LANES) == (jnp.arange(LANES)[None, :] % SUB)).astype(BF16)
    in_specs = [
        pl.BlockSpec((NBLK, NS, C, LANES), lambda b, c: (0, b, c, 0)),
        per_b((HG_HEADS, LANES, LANES)),
        per_b((RW_PAIRS, LANES, LANES)),
        per_b((RW_BLK, 1, LANES)),
        full((1, HG_WIDTH)), full((1, HG_WIDTH)), full((RW_BLK, 1, LANES)),
        full((1, RW_WIDTH)), full((1, RW_WIDTH)), full((LANES, 2 * RW_WIDTH)),
        full((1, RW_WIDTH)), full((1, RW_WIDTH)), full((1, RW_WIDTH)),
        full((1, RW_WIDTH)), full((1, RW_WIDTH)), full((SUB * LANES, LANES)),
    ]
    out_specs = [
        pl.BlockSpec((NS, C, D_MODEL), lambda b, c: (b, c, 0)),
        per_b((HG_HEADS, LANES, LANES)),
        per_b((RW_PAIRS, LANES, LANES)),
        per_b((RW_BLK, 1, LANES)),
    ]
    out_shape = [
        jax.ShapeDtypeStruct((B, T, D_MODEL), BF16),
        jax.ShapeDtypeStruct((B, HG_HEADS, LANES, LANES), F32),
        jax.ShapeDtypeStruct((B, RW_PAIRS, LANES, LANES), F32),
        jax.ShapeDtypeStruct((B, RW_BLK, 1, LANES), F32),
    ]
    scratch = ([pltpu.VMEM((NS, C, HG_WIDTH), F32)] * 8
               + [pltpu.VMEM((NS, C, 2 * RW_WIDTH), F32)])
    mixed, hg, rw, sh = pl.pallas_call(
        functools.partial(_mixer_kernel, C=C, NS=NS, stagger=STAGGER),
        grid=(B // NS, nc),
        in_specs=in_specs,
        out_specs=out_specs,
        out_shape=out_shape,
        scratch_shapes=scratch,
        compiler_params=pltpu.CompilerParams(
            dimension_semantics=("arbitrary", "arbitrary"), vmem_limit_bytes=VMEM_LIMIT),
        name="mixer",
    )(proj.reshape(NBLK, B, T, LANES), hg0, rw0, sh0, lb, hgn, mu, w0, a0, w2a, kk, ka, rk, lnw, lnb, wsel)
    return mixed.reshape(B * T, D_MODEL), hg, rw, sh


def _outproj_kernel(m_ref, w_ref, g_ref, x_ref, o_ref):
    z = jnp.dot(m_ref[...], w_ref[...], preferred_element_type=F32)
    ms = jnp.mean(z * z, axis=-1, keepdims=True)
    o_ref[...] = x_ref[...] + z * lax.rsqrt(ms + NORM_EPS) * g_ref[...]


def _outproj(mixed, w_bf, g, x, tm):
    rows = x.shape[0]
    return pl.pallas_call(
        _outproj_kernel,
        grid=(rows // tm,),
        in_specs=[
            pl.BlockSpec((tm, D_MODEL), lambda i: (i, 0)),
            pl.BlockSpec((D_MODEL, D_MODEL), lambda i: (0, 0)),
            pl.BlockSpec((1, D_MODEL), lambda i: (0, 0)),
            pl.BlockSpec((tm, D_MODEL), lambda i: (i, 0)),
        ],
        out_specs=pl.BlockSpec((tm, D_MODEL), lambda i: (i, 0)),
        out_shape=jax.ShapeDtypeStruct((rows, D_MODEL), F32),
        compiler_params=pltpu.CompilerParams(
            dimension_semantics=("arbitrary",), vmem_limit_bytes=VMEM_LIMIT),
        name="outproj",
    )(mixed, w_bf, g, x)


def _rw_pack(s):
    B = s.shape[0]
    s = s.reshape(B, RW_PAIRS, 2, RW_HEAD, RW_HEAD)
    z = jnp.zeros_like(s[:, :, 0])
    top = jnp.concatenate([s[:, :, 0], z], axis=-1)
    bot = jnp.concatenate([z, s[:, :, 1]], axis=-1)
    return jnp.concatenate([top, bot], axis=-2)


def _rw_unpack(s):
    B = s.shape[0]
    a = s[:, :, :RW_HEAD, :RW_HEAD]
    b = s[:, :, RW_HEAD:, RW_HEAD:]
    return jnp.stack([a, b], axis=2).reshape(B, 2 * RW_PAIRS, RW_HEAD, RW_HEAD)


def _pick_tile(rows, cap):
    t = cap
    while rows % t:
        t //= 2
    return t


def _trunk(x, hg0, rw0, sh0, weights, C, need_y=True):
    (norm_pre, w_in_bf, mixer_params, w_out_bf, norm_post) = weights
    B, T, _ = x.shape
    x2 = x.reshape(B * T, D_MODEL)
    proj = _inproj(x2, norm_pre, w_in_bf, _pick_tile(B * T, 1024))
    mixed, hg, rw, sh = _mixer(proj, hg0, rw0, sh0, mixer_params, B, T, C)
    y = None
    if need_y:
        y = _outproj(mixed, w_out_bf, norm_post, x2, _pick_tile(B * T, 512)).reshape(B, T, D_MODEL)
    return y, hg, rw, sh


def kernel(x_prompt, x_sample, state_hgrn, state_rwkv, state_shift, meta_tokens, norm_pre, w_in,
           hg_lower_bounds, hg_norm, rw_mu, rw_w0, rw_w2, rw_a0, rw_a2, rw_k_k, rw_k_a, rw_r_k,
           rw_ln_w, rw_ln_b, w_out, norm_post):
    B = x_prompt.shape[0]
    Bs = x_sample.shape[0]
    n_meta = meta_tokens.shape[0]
    l = 0

    lbs = jnp.cumsum(jax.nn.softmax(hg_lower_bounds.astype(F32), axis=0), axis=0)[l].reshape(1, HG_WIDTH)
    z64 = jnp.zeros((RW_HEAD, RW_WIDTH), F32)
    w2a = jnp.concatenate([jnp.concatenate([rw_w2[l], z64], axis=1),
                           jnp.concatenate([z64, rw_a2[l]], axis=1)], axis=0).astype(BF16)
    row = lambda t: t.reshape(1, -1).astype(F32)
    mixer_params = (lbs, row(hg_norm[l]), rw_mu[l].reshape(RW_BLK, 1, LANES), row(rw_w0[l]), row(rw_a0[l]),
                    w2a, row(rw_k_k[l]), row(rw_k_a[l]), row(rw_r_k[l]), row(rw_ln_w[l]), row(rw_ln_b[l]))
    weights = (row(norm_pre[l]), w_in[l].astype(BF16), mixer_params, w_out[l].astype(BF16), row(norm_post[l]))

    zero_hg = jnp.zeros((1, HG_HEADS, LANES, LANES), F32)
    zero_rw = jnp.zeros((1, RW_PAIRS, LANES, LANES), F32)
    zero_sh = jnp.zeros((1, RW_BLK, 1, LANES), F32)
    _, hg_m, rw_m, sh_m = _trunk(meta_tokens[None].astype(F32), zero_hg, zero_rw, zero_sh, weights,
                                 C=n_meta, need_y=False)

    rep = lambda t: jnp.broadcast_to(t, (B,) + t.shape[1:])
    y_p, hg_p, rw_p, sh_p = _trunk(x_prompt, rep(hg_m), rep(rw_m), rep(sh_m), weights, C=CHUNK)

    hg_s0 = jnp.swapaxes(state_hgrn[l], -1, -2)
    rw_s0 = _rw_pack(state_rwkv[l])
    sh_s0 = state_shift[l].reshape(Bs, RW_BLK, 1, LANES)
    y_s, hg_s, rw_s, sh_s = _trunk(x_sample, hg_s0, rw_s0, sh_s0, weights, C=CHUNK)

    unhg = lambda t: jnp.swapaxes(t, -1, -2)[None]
    unsh = lambda t: t.reshape(t.shape[0], 1, RW_BLK * LANES)[None]
    return (y_p, y_s,
            unhg(hg_p), _rw_unpack(rw_p)[None], unsh(sh_p),
            unhg(hg_s), _rw_unpack(rw_s)[None], unsh(sh_s))
```

```python
import functools

import jax
import jax.numpy as jnp
from jax import lax
from jax.experimental import pallas as pl
from jax.experimental.pallas import tpu as pltpu

F32 = jnp.float32
BF16 = jnp.bfloat16

LANES = 128
SUB = 8
D_MODEL = 2048
HG_HEADS = 8
HG_WIDTH = 1024
RW_PAIRS = 8
RW_HEAD = 64
RW_WIDTH = 1024
P_TOTAL = 8320
NBLK = P_TOTAL // LANES
HG_BLK = 32
RW_BLK = NBLK - HG_BLK
NORM_EPS = 1e-6
RW_GN_EPS = 64e-5
KK_EPS = 1e-12
VMEM_LIMIT = 56 * 1024 * 1024
LOG2E = 1.4426950408889634
CHUNK = 64
STREAMS_PER_STEP = 2
STAGGER = 1


def _bf(x):
    return x.astype(BF16)


def _dot(a, b):
    return jnp.dot(_bf(a), _bf(b), preferred_element_type=F32)


def _dot_nt(a, b):
    return lax.dot_general(_bf(a), _bf(b), (((1,), (1,)), ((), ())), preferred_element_type=F32)


def _dot_tn(a, b):
    return lax.dot_general(_bf(a), _bf(b), (((0,), (0,)), ((), ())), preferred_element_type=F32)


def _dot_exact_lhs(l_bf, x):
    hi = _bf(x)
    lo = _bf(x - hi.astype(F32))
    return jnp.dot(jnp.concatenate([l_bf, l_bf], axis=1), jnp.concatenate([hi, lo], axis=0),
                   preferred_element_type=F32)


def _sigmoid(x):
    return 1.0 / (1.0 + jnp.exp2(x * (-LOG2E)))


def _inproj_kernel(x_ref, g_ref, w_ref, o_ref, u_ref, *, nb):
    @pl.when(pl.program_id(1) == 0)
    def _():
        x = x_ref[...]
        ms = jnp.mean(x * x, axis=-1, keepdims=True)
        u_ref[...] = _bf(x * lax.rsqrt(ms + NORM_EPS) * g_ref[...])

    acc = jnp.dot(u_ref[...], w_ref[...], preferred_element_type=F32)
    for cb in range(nb):
        o_ref[cb] = acc[:, cb * LANES:(cb + 1) * LANES]


def _inproj(x, g, w_bf, tm):
    rows = x.shape[0]
    nb = 13
    tn = nb * LANES
    return pl.pallas_call(
        functools.partial(_inproj_kernel, nb=nb),
        grid=(rows // tm, NBLK // nb),
        in_specs=[
            pl.BlockSpec((tm, D_MODEL), lambda i, j: (i, 0)),
            pl.BlockSpec((1, D_MODEL), lambda i, j: (0, 0)),
            pl.BlockSpec((D_MODEL, tn), lambda i, j: (0, j)),
        ],
        out_specs=pl.BlockSpec((nb, tm, LANES), lambda i, j: (j, i, 0)),
        out_shape=jax.ShapeDtypeStruct((NBLK, rows, LANES), F32),
        scratch_shapes=[pltpu.VMEM((tm, D_MODEL), BF16)],
        compiler_params=pltpu.CompilerParams(
            dimension_semantics=("arbitrary", "arbitrary"), vmem_limit_bytes=VMEM_LIMIT),
        name="inproj",
    )(x, g, w_bf)


def _mixer_kernel(proj_ref, hg0_ref, rw0_ref, sh0_ref, lb_ref, hgn_ref, mu_ref, w0_ref, a0_ref,
                  w2a_ref, kk_ref, ka_ref, rk_ref, lnw_ref, lnb_ref, wsel_ref,
                  out_ref, hgS_ref, rw_out_ref, sh_ref,
                  q_all, k_all, g_all, lk_all, gk_all, lw_all, gam_all, cum_all, lora_all, rwS_ref,
                  *, C, NS, stagger):
    @pl.when(pl.program_id(1) == 0)
    def _():
        for s in range(NS):
            for h in range(HG_HEADS):
                hgS_ref[s, h] = hg0_ref[s, h].T
        zero = jnp.zeros((RW_HEAD, RW_HEAD), F32)
        for s in range(NS):
            for p in range(RW_PAIRS):
                rwS_ref[s, p] = jnp.concatenate(
                    [jnp.concatenate([rw0_ref[s, 2 * p], zero], axis=1),
                     jnp.concatenate([zero, rw0_ref[s, 2 * p + 1]], axis=1)], axis=0)
        sh_ref[...] = sh0_ref[...]

    row = lax.broadcasted_iota(jnp.int32, (C, LANES), 0)
    lane = lax.broadcasted_iota(jnp.int32, (C, LANES), 1)
    ri = lax.broadcasted_iota(jnp.int32, (C, C), 0)
    ci = lax.broadcasted_iota(jnp.int32, (C, C), 1)
    tril_bf = (ci <= ri).astype(F32).astype(BF16)
    levels = [b for b in (8, 16, 32) if 2 * b <= C]
    level_sign = {b: jnp.where((row & b) != 0, 1.0, -1.0) for b in levels}
    level_qk = {b: functools.partial(jnp.where, (row & b) != 0) for b in levels}
    level_keep = {b: ((ri & b) != 0) & ((ci & b) == 0) & ((ri & (-2 * b)) == (ci & (-2 * b))) for b in levels}
    diag_keep = ((lane >> 3) == (row >> 3)) & (lane <= row)
    head0 = lane < RW_HEAD
    blockdiag = ((lax.broadcasted_iota(jnp.int32, (LANES, LANES), 0) < RW_HEAD)
                 == (lax.broadcasted_iota(jnp.int32, (LANES, LANES), 1) < RW_HEAD))
    ones_bf = blockdiag.astype(F32).astype(BF16)

    def group_sums(xs):
        s = jnp.dot(_bf(jnp.concatenate(xs, axis=0)), ones_bf, preferred_element_type=F32)
        return [s[i * C:(i + 1) * C] for i in range(len(xs))]
    C2 = 2 * C
    aligned = C2 % LANES == 0
    ri2 = lax.broadcasted_iota(jnp.int32, (C, C2), 0)
    ci2 = lax.broadcasted_iota(jnp.int32, (C, C2), 1)
    cj2 = jnp.where(ci2 >= C, ci2 - C, ci2)
    smask = cj2 < ri2
    imask = cj2 <= ri2
    first = ci2 < C
    eye2 = (lax.broadcasted_iota(jnp.int32, (C2, C2), 0)
            == lax.broadcasted_iota(jnp.int32, (C2, C2), 1)).astype(F32)
    n_dbl = C.bit_length() - 1
    half = HG_HEADS // 2
    pairs = range(RW_PAIRS)
    sls = [slice(p * LANES, (p + 1) * LANES) for p in pairs]

    h0 = head0.astype(F32).astype(BF16)
    h1 = (~head0).astype(F32).astype(BF16)

    def by_head(x):
        xb = _bf(x)
        return jnp.concatenate([xb * h0, xb * h1], axis=0)

    def two_rows(ref, base, hs):
        return jnp.concatenate([jnp.broadcast_to(ref[base:base + 1, hs], (SUB, LANES)),
                                jnp.broadcast_to(ref[base + SUB:base + SUB + 1, hs], (SUB, LANES))], axis=0)

    def token_shift(s, blk):
        p = proj_ref[HG_BLK + blk, s]
        prev = pltpu.roll(p, 1, axis=0)
        prev = jnp.concatenate([jnp.where(row[:SUB] == 0, sh_ref[s, blk], prev[:SUB]), prev[SUB:]], axis=0)
        return p + (prev - p) * mu_ref[blk]

    xl = jnp.concatenate([token_shift(s, RW_BLK - 1) for s in range(NS)], axis=0)
    xl = jnp.where(jnp.concatenate([lane] * NS, axis=0) < RW_HEAD, jnp.tanh(xl), xl)
    lora = jnp.dot(_bf(xl), w2a_ref[...], preferred_element_type=F32)
    for s in range(NS):
        lora_all[s] = lora[s * C:(s + 1) * C]

    def program(s):
        proj = proj_ref.at[:, s]
        out = out_ref.at[s]
        hgS, rwS, sh = hgS_ref.at[s], rwS_ref.at[s], sh_ref.at[s]
        q_s, k_s, g_s, lk_s, gk_s = q_all.at[s], k_all.at[s], g_all.at[s], lk_all.at[s], gk_all.at[s]
        lw_s, gam_s, cum_s, lora_s = lw_all.at[s], gam_all.at[s], cum_all.at[s], lora_all.at[s]

        def hg_gates(_):
            for h in range(HG_HEADS):
                hs = slice(h * LANES, (h + 1) * LANES)
                pq = proj[h]
                lb = lb_ref[:, hs]
                forget = lb + (1.0 - lb) * _sigmoid(proj[HG_HEADS + h])
                k = 1.0 - forget
                q_s[:, hs] = pq * _sigmoid(pq)
                k_s[:, hs] = k
                lk_s[:, hs] = jnp.log2(k)
                g_s[:, hs] = jnp.log2(forget)
            cum_s[...] = _dot_exact_lhs(tril_bf, g_s[...])

        scores_off = [None] * HG_HEADS
        diag_lhs = [None] * HG_HEADS
        diags = {}

        def hg_scores(h):
            hs = slice(h * LANES, (h + 1) * LANES)
            q = q_s[:, hs]
            k = k_s[:, hs]
            lk = lk_s[:, hs]
            G = cum_s[0:C, hs]
            gk_s[:, hs] = G - lk
            scores = jnp.zeros((C, C), F32)
            for b in levels:
                gref = jnp.concatenate(
                    [jnp.broadcast_to(cum_s[2 * b * blk + b - 1:2 * b * blk + b, hs], (2 * b, LANES))
                     for blk in range(C // (2 * b))], axis=0)
                me = _bf(level_qk[b](q, k) * jnp.exp2((G - gref) * level_sign[b]))
                sc = lax.dot_general(me, me, (((1,), (1,)), ((), ())), preferred_element_type=F32)
                scores = jnp.where(level_keep[b], sc, scores)
            scores_off[h] = scores

            units = []
            for u in range(C // (2 * SUB)):
                r0 = u * 2 * SUB
                G16 = G[r0:r0 + 2 * SUB]
                q16 = _bf(q[r0:r0 + 2 * SUB])
                cols = []
                for j in range(SUB):
                    x = jnp.minimum(G16 - two_rows(gk_s, r0 + j, hs), two_rows(lk_s, r0 + j, hs))
                    cols.append(q16 * _bf(jnp.exp2(x)))
                units.append(jnp.concatenate(cols, axis=1))
            diag_lhs[h] = jnp.concatenate(units, axis=0)

        def hg_diag(part):
            lhs = jnp.concatenate(diag_lhs[part * half:(part + 1) * half], axis=0)
            diags[part] = jnp.dot(lhs, wsel_ref[...], preferred_element_type=F32)

        def hg_finish(h):
            hs = slice(h * LANES, (h + 1) * LANES)
            diag_h = diags[h // half][(h % half) * C:(h % half + 1) * C]
            G = cum_s[0:C, hs]
            v = proj[2 * HG_HEADS + h]
            pg = proj[3 * HG_HEADS + h]
            St = hgS[h]
            scores = scores_off[h] + jnp.where(diag_keep, diag_h, 0.0)[:, :C]
            o = _dot_nt(q_s[:, hs] * jnp.exp2(G), St) + _dot(scores, v)

            Gl = G[C - 1:C, :]
            hgS[h] = St * jnp.exp2(Gl) + _dot_tn(v, jnp.exp2(Gl - gk_s[:, hs]))

            o = o * lax.rsqrt(jnp.mean(o * o, axis=-1, keepdims=True) + NORM_EPS)
            o = o * hgn_ref[:, hs] * (pg * _sigmoid(pg))
            out[:, hs] = o.astype(out.dtype)

        S_, F_, D_ = hg_scores, hg_finish, hg_diag
        fillers = [(hg_gates, 0), (S_, 0), (S_, 1), (S_, 2), (S_, 3), (D_, 0), (S_, 4), (S_, 5), (F_, 0), (F_, 1),
                   (S_, 6), (F_, 2), (F_, 3), (S_, 7), (D_, 1), (F_, 4), (F_, 5), (F_, 6), (F_, 7)]

        def fill(n):
            for _ in range(min(n, len(fillers))):
                fn, arg = fillers.pop(0)
                fn(arg)

        shifted = functools.partial(token_shift, s)
        st = [dict() for _ in pairs]

        for p, d in zip(pairs, st):
            sl = sls[p]
            d["r"] = shifted(p)
            d["xk"] = shifted(RW_PAIRS + p)
            d["v"] = shifted(2 * RW_PAIRS + p)
            w = w0_ref[:, sl] + lora_s[:, sl]
            lw_s[:, sl] = (-0.6065306597126334 * LOG2E) * _sigmoid(w)
            d["a"] = _sigmoid(a0_ref[:, sl] + lora_s[:, RW_WIDTH + p * LANES:RW_WIDTH + (p + 1) * LANES])
            d["kkr"] = d["xk"] * kk_ref[:, sl]
            d["k"] = d["xk"] * (1.0 + (d["a"] - 1.0) * ka_ref[:, sl])
        yield
        for d, ss in zip(st, group_sums([d["kkr"] * d["kkr"] for d in st])):
            d["kk"] = d["kkr"] * lax.rsqrt(jnp.maximum(ss, KK_EPS * KK_EPS))
            d["b"] = d["kk"] * d["a"]
        for d, bonus in zip(st, group_sums([d["r"] * d["k"] * rk_ref[:, sls[p]] for p, d in zip(pairs, st)])):
            d["bonus"] = bonus

        gam_s[...] = _dot_exact_lhs(tril_bf, lw_s[...])
        yield

        for p, d in zip(pairs, st):
            sl = sls[p]
            gam = gam_s[:, sl]
            eg = jnp.exp2(gam)
            eng = 1.0 / eg
            at = -d["kk"] * jnp.exp2(gam - lw_s[:, sl])
            rt = d["r"] * eg
            bt = d["b"] * eng
            kt = d["k"] * eng
            dl = eg[C - 1:C, :] * eng
            d["bk_dl"] = _bf(jnp.concatenate([d["b"] * dl, d["k"] * dl], axis=0))
            d["eg_last"] = eg[C - 1:C, :]
            d["S"] = rwS[p]
            at_rt = _bf(jnp.concatenate([at, rt], axis=0))
            bt, kt = _bf(bt), _bf(kt)
            rhs0 = jnp.concatenate([bt * h0, kt * h0], axis=0)
            rhs1 = jnp.concatenate([kt * h1, bt * h1], axis=0)
            if aligned:
                m = _dot_nt(at_rt, jnp.concatenate([rhs0, rhs1, _bf(d["S"])], axis=0))
                m0, m1, d["wst"] = m[:, :C2], m[:, C2:2 * C2], m[:, 2 * C2:]
            else:
                m0, m1, d["wst"] = _dot_nt(at_rt, rhs0), _dot_nt(at_rt, rhs1), _dot_nt(at_rt, d["S"])
            aa0 = jnp.where(smask, m0[:C], 0.0)
            aa1 = jnp.where(smask, m1[:C], 0.0)
            d["ak"] = _bf(jnp.where(first, aa1, aa0))
            d["ar0"] = _bf(jnp.where(imask, m0[C:], 0.0))
            d["ar1"] = _bf(jnp.where(imask, m1[C:], 0.0))
            pw = jnp.concatenate([jnp.where(first, aa0, 0.0), jnp.where(first, 0.0, aa1)], axis=0)
            d["t"] = eye2 + pw
            d["pw"] = _bf(pw)
        fill(1)
        yield

        for d in st:
            d["pw"] = _bf(_dot(d["pw"], d["pw"]))
        fill(1)
        yield
        for _ in range(n_dbl - 2):
            for d in st:
                if aligned:
                    res = _dot(d["pw"], jnp.concatenate([d["pw"], _bf(d["t"])], axis=1))
                    d["pw"], d["t"] = _bf(res[:, :C2]), d["t"] + res[:, C2:]
                else:
                    d["pw"], d["t"] = _bf(_dot(d["pw"], d["pw"])), d["t"] + _dot(d["pw"], d["t"])
            fill(1)
            yield

        for d in st:
            vb = _bf(d["v"])
            d["v0"], d["v1"] = vb * h0, vb * h1
            akv = _dot(d["ak"], jnp.concatenate([d["v1"], d["v0"]], axis=0))
            d["rhs2"] = by_head(d["wst"][:C] + akv)
        fill(1)
        yield
        for d in st:
            d["w"] = _dot(d["t"][:C] + d["t"][C:], d["rhs2"])
        fill(3)
        yield
        for d in st:
            d["u"] = d["w"] + _dot(d["pw"][:C] + d["pw"][C:], by_head(d["w"]))
        fill(3)
        yield

        for p, d in zip(pairs, st):
            u, v = d["u"], d["v"]
            ub = _bf(u)
            u0, u1 = ub * h0, ub * h1
            if aligned:
                yuv = _dot(jnp.concatenate([d["ar0"], d["ar1"]], axis=1),
                           jnp.concatenate([u0, d["v0"], d["v1"], u1], axis=0))
            else:
                yuv = (_dot(d["ar0"], jnp.concatenate([u0, d["v0"]], axis=0))
                       + _dot(d["ar1"], jnp.concatenate([d["v1"], u1], axis=0)))
            d["y"] = d["wst"][C:] + yuv
            upd = _dot_tn(jnp.concatenate([ub, _bf(v)], axis=0), d["bk_dl"])
            rwS[p] = d["S"] * d["eg_last"] + jnp.where(blockdiag, upd, 0.0)
        fill(1)
        yield

        for d, m in zip(st, group_sums([d["y"] for d in st])):
            d["yc"] = d["y"] - m * (1.0 / RW_HEAD)
        fill(1)
        yield
        for d, var in zip(st, group_sums([d["yc"] * d["yc"] for d in st])):
            d["var"] = var
        fill(2)
        yield
        for p, d in zip(pairs, st):
            sl = sls[p]
            yn = d["yc"] * lax.rsqrt(d["var"] * (1.0 / RW_HEAD) + RW_GN_EPS) * lnw_ref[:, sl] + lnb_ref[:, sl]
            xg = shifted(3 * RW_PAIRS + p)
            yo = (yn + d["bonus"] * d["v"]) * (xg * _sigmoid(xg))
            out[:, HG_WIDTH + p * LANES:HG_WIDTH + (p + 1) * LANES] = yo.astype(out.dtype)
        fill(len(fillers))

        for blk in range(RW_BLK):
            sh[blk] = proj[HG_BLK + blk, C - 1:C, :]

    todo = [(s * stagger, program(s)) for s in range(NS)]
    rnd = 0
    while todo:
        todo = [(t0, g) for t0, g in todo if rnd < t0 or next(g, StopIteration) is not StopIteration]
        rnd += 1

    @pl.when(pl.program_id(1) == pl.num_programs(1) - 1)
    def _():
        for s in range(NS):
            for h in range(HG_HEADS):
                hgS_ref[s, h] = hgS_ref[s, h].T
            for p in range(RW_PAIRS):
                S = rwS_ref[s, p]
                rw_out_ref[s, 2 * p] = S[:RW_HEAD, :RW_HEAD]
                rw_out_ref[s, 2 * p + 1] = S[RW_HEAD:, RW_HEAD:]


def _mixer(proj, hg0, rw0, sh0, params, B, T, C):
    nc = T // C
    NS = STREAMS_PER_STEP if B % STREAMS_PER_STEP == 0 else 1
    full = lambda shape: pl.BlockSpec(shape, lambda b, c: (0,) * len(shape))
    per_b = lambda shape: pl.BlockSpec((NS,) + shape, lambda b, c: (b,) + (0,) * len(shape))
    lb, hgn, mu, w0, a0, w2a, kk, ka, rk, lnw, lnb = params
    wsel = ((jnp.arange(SUB * LANES)[:, None] // LANES) == (jnp.arange(LANES)[None, :] % SUB)).astype(BF16)
    in_specs = [
        pl.BlockSpec((NBLK, NS, C, LANES), lambda b, c: (0, b, c, 0)),
        per_b((HG_HEADS, LANES, LANES)),
        per_b((2 * RW_PAIRS, RW_HEAD, RW_HEAD)),
        per_b((RW_BLK, 1, LANES)),
        full((1, HG_WIDTH)), full((1, HG_WIDTH)), full((RW_BLK, 1, LANES)),
        full((1, RW_WIDTH)), full((1, RW_WIDTH)), full((LANES, 2 * RW_WIDTH)),
        full((1, RW_WIDTH)), full((1, RW_WIDTH)), full((1, RW_WIDTH)),
        full((1, RW_WIDTH)), full((1, RW_WIDTH)), full((SUB * LANES, LANES)),
    ]
    out_specs = [
        pl.BlockSpec((NS, C, D_MODEL), lambda b, c: (b, c, 0)),
        per_b((HG_HEADS, LANES, LANES)),
        per_b((2 * RW_PAIRS, RW_HEAD, RW_HEAD)),
        per_b((RW_BLK, 1, LANES)),
    ]
    out_shape = [
        jax.ShapeDtypeStruct((B, T, D_MODEL), BF16),
        jax.ShapeDtypeStruct((B, HG_HEADS, LANES, LANES), F32),
        jax.ShapeDtypeStruct((B, 2 * RW_PAIRS, RW_HEAD, RW_HEAD), F32),
        jax.ShapeDtypeStruct((B, RW_BLK, 1, LANES), F32),
    ]
    scratch = ([pltpu.VMEM((NS, C, HG_WIDTH), F32)] * 8
               + [pltpu.VMEM((NS, C, 2 * RW_WIDTH), F32)]
               + [pltpu.VMEM((NS, RW_PAIRS, LANES, LANES), F32)])
    mixed, hg, rw, sh = pl.pallas_call(
        functools.partial(_mixer_kernel, C=C, NS=NS, stagger=STAGGER),
        grid=(B // NS, nc),
        in_specs=in_specs,
        out_specs=out_specs,
        out_shape=out_shape,
        scratch_shapes=scratch,
        compiler_params=pltpu.CompilerParams(
            dimension_semantics=("arbitrary", "arbitrary"), vmem_limit_bytes=VMEM_LIMIT),
        name="mixer",
    )(proj.reshape(NBLK, B, T, LANES), hg0, rw0, sh0, lb, hgn, mu, w0, a0, w2a, kk, ka, rk, lnw, lnb, wsel)
    return mixed.reshape(B * T, D_MODEL), hg, rw, sh


def _outproj_kernel(m_ref, w_ref, g_ref, x_ref, o_ref):
    z = jnp.dot(m_ref[...], w_ref[...], preferred_element_type=F32)
    ms = jnp.mean(z * z, axis=-1, keepdims=True)
    o_ref[...] = x_ref[...] + z * lax.rsqrt(ms + NORM_EPS) * g_ref[...]


def _outproj(mixed, w_bf, g, x, tm):
    rows = x.shape[0]
    return pl.pallas_call(
        _outproj_kernel,
        grid=(rows // tm,),
        in_specs=[
            pl.BlockSpec((tm, D_MODEL), lambda i: (i, 0)),
            pl.BlockSpec((D_MODEL, D_MODEL), lambda i: (0, 0)),
            pl.BlockSpec((1, D_MODEL), lambda i: (0, 0)),
            pl.BlockSpec((tm, D_MODEL), lambda i: (i, 0)),
        ],
        out_specs=pl.BlockSpec((tm, D_MODEL), lambda i: (i, 0)),
        out_shape=jax.ShapeDtypeStruct((rows, D_MODEL), F32),
        compiler_params=pltpu.CompilerParams(
            dimension_semantics=("arbitrary",), vmem_limit_bytes=VMEM_LIMIT),
        name="outproj",
    )(mixed, w_bf, g, x)


def _pick_tile(rows, cap):
    t = cap
    while rows % t:
        t //= 2
    return t


def _trunk(x, hg0, rw0, sh0, weights, C, need_y=True):
    (norm_pre, w_in_bf, mixer_params, w_out_bf, norm_post) = weights
    B, T, _ = x.shape
    x2 = x.reshape(B * T, D_MODEL)
    proj = _inproj(x2, norm_pre, w_in_bf, _pick_tile(B * T, 1024))
    mixed, hg, rw, sh = _mixer(proj, hg0, rw0, sh0, mixer_params, B, T, C)
    y = None
    if need_y:
        y = _outproj(mixed, w_out_bf, norm_post, x2, _pick_tile(B * T, 512)).reshape(B, T, D_MODEL)
    return y, hg, rw, sh


def kernel(x_prompt, x_sample, state_hgrn, state_rwkv, state_shift, meta_tokens, norm_pre, w_in,
           hg_lower_bounds, hg_norm, rw_mu, rw_w0, rw_w2, rw_a0, rw_a2, rw_k_k, rw_k_a, rw_r_k,
           rw_ln_w, rw_ln_b, w_out, norm_post):
    B = x_prompt.shape[0]
    Bs = x_sample.shape[0]
    n_meta = meta_tokens.shape[0]
    l = 0

    lbs = jnp.cumsum(jax.nn.softmax(hg_lower_bounds.astype(F32), axis=0), axis=0)[l].reshape(1, HG_WIDTH)
    z64 = jnp.zeros((RW_HEAD, RW_WIDTH), F32)
    w2a = jnp.concatenate([jnp.concatenate([rw_w2[l], z64], axis=1),
                           jnp.concatenate([z64, rw_a2[l]], axis=1)], axis=0).astype(BF16)
    row = lambda t: t.reshape(1, -1).astype(F32)
    mixer_params = (lbs, row(hg_norm[l]), rw_mu[l].reshape(RW_BLK, 1, LANES), row(rw_w0[l]), row(rw_a0[l]),
                    w2a, row(rw_k_k[l]), row(rw_k_a[l]), row(rw_r_k[l]), row(rw_ln_w[l]), row(rw_ln_b[l]))
    weights = (row(norm_pre[l]), w_in[l].astype(BF16), mixer_params, w_out[l].astype(BF16), row(norm_post[l]))

    zero_hg = jnp.zeros((1, HG_HEADS, LANES, LANES), F32)
    zero_rw = jnp.zeros((1, 2 * RW_PAIRS, RW_HEAD, RW_HEAD), F32)
    zero_sh = jnp.zeros((1, RW_BLK, 1, LANES), F32)
    _, hg_m, rw_m, sh_m = _trunk(meta_tokens[None].astype(F32), zero_hg, zero_rw, zero_sh, weights,
                                 C=n_meta, need_y=False)

    rep = lambda t: jnp.broadcast_to(t, (B,) + t.shape[1:])
    y_p, hg_p, rw_p, sh_p = _trunk(x_prompt, rep(hg_m), rep(rw_m), rep(sh_m), weights, C=CHUNK)

    sh_s0 = state_shift[l].reshape(Bs, RW_BLK, 1, LANES)
    y_s, hg_s, rw_s, sh_s = _trunk(x_sample, state_hgrn[l], state_rwkv[l], sh_s0, weights, C=CHUNK)

    unsh = lambda t: t.reshape(t.shape[0], 1, RW_BLK * LANES)[None]
    return (y_p, y_s,
            hg_p[None], rw_p[None], unsh(sh_p),
            hg_s[None], rw_s[None], unsh(sh_s))
```

```python
import functools

import jax
import jax.numpy as jnp
from jax import lax
from jax.experimental import pallas as pl
from jax.experimental.pallas import tpu as pltpu

F32 = jnp.float32
BF16 = jnp.bfloat16

LANES = 128
SUB = 8
D_MODEL = 2048
HG_HEADS = 8
HG_WIDTH = 1024
RW_PAIRS = 8
RW_HEAD = 64
RW_WIDTH = 1024
P_TOTAL = 8320
NBLK = P_TOTAL // LANES
HG_BLK = 32
RW_BLK = NBLK - HG_BLK
NORM_EPS = 1e-6
RW_GN_EPS = 64e-5
KK_EPS = 1e-12
VMEM_LIMIT = 56 * 1024 * 1024
LOG2E = 1.4426950408889634
RW_DECAY_SCALE = 0.6065306597126334
CHUNK = 64
STREAMS_PER_STEP = 2
STAGGER = 1
INPROJ_ROWS = 1024
INPROJ_COL_BLOCKS = 13
OUTPROJ_ROWS = 512


def _bf(x):
    return x.astype(BF16)


def _dot(a, b):
    return jnp.dot(_bf(a), _bf(b), preferred_element_type=F32)


def _dot_nt(a, b):
    return lax.dot_general(_bf(a), _bf(b), (((1,), (1,)), ((), ())), preferred_element_type=F32)


def _dot_tn(a, b):
    return lax.dot_general(_bf(a), _bf(b), (((0,), (0,)), ((), ())), preferred_element_type=F32)


def _dot_exact_lhs(l_bf, x):
    hi = _bf(x)
    lo = _bf(x - hi.astype(F32))
    return jnp.dot(jnp.concatenate([l_bf, l_bf], axis=1), jnp.concatenate([hi, lo], axis=0),
                   preferred_element_type=F32)


def _sigmoid(x):
    return 1.0 / (1.0 + jnp.exp2(x * (-LOG2E)))


def _inproj_kernel(x_ref, g_ref, w_ref, o_ref, u_ref, *, nb):
    @pl.when(pl.program_id(1) == 0)
    def _():
        x = x_ref[...]
        ms = jnp.mean(x * x, axis=-1, keepdims=True)
        u_ref[...] = _bf(x * lax.rsqrt(ms + NORM_EPS) * g_ref[...])

    acc = jnp.dot(u_ref[...], w_ref[...], preferred_element_type=F32)
    for cb in range(nb):
        o_ref[cb] = acc[:, cb * LANES:(cb + 1) * LANES]


def _inproj(x, g, w_bf, tm):
    rows = x.shape[0]
    nb = INPROJ_COL_BLOCKS
    tn = nb * LANES
    return pl.pallas_call(
        functools.partial(_inproj_kernel, nb=nb),
        grid=(rows // tm, NBLK // nb),
        in_specs=[
            pl.BlockSpec((tm, D_MODEL), lambda i, j: (i, 0)),
            pl.BlockSpec((1, D_MODEL), lambda i, j: (0, 0)),
            pl.BlockSpec((D_MODEL, tn), lambda i, j: (0, j)),
        ],
        out_specs=pl.BlockSpec((nb, tm, LANES), lambda i, j: (j, i, 0)),
        out_shape=jax.ShapeDtypeStruct((NBLK, rows, LANES), F32),
        scratch_shapes=[pltpu.VMEM((tm, D_MODEL), BF16)],
        compiler_params=pltpu.CompilerParams(
            dimension_semantics=("arbitrary", "arbitrary"), vmem_limit_bytes=VMEM_LIMIT),
        name="inproj",
    )(x, g, w_bf)


def _mixer_kernel(proj_ref, hg0_ref, rw0_ref, sh0_ref, lb_ref, hgn_ref, mu_ref, w0_ref, a0_ref,
                  w2a_ref, kk_ref, ka_ref, rk_ref, lnw_ref, lnb_ref, wsel_ref,
                  out_ref, hgS_ref, rw_out_ref, sh_ref,
                  q_all, k_all, g_all, lk_all, gk_all, lw_all, gam_all, cum_all, lora_all, rwS_ref,
                  *, C, NS, stagger):
    @pl.when(pl.program_id(1) == 0)
    def _():
        for s in range(NS):
            for h in range(HG_HEADS):
                hgS_ref[s, h] = hg0_ref[s, h].T
        zero = jnp.zeros((RW_HEAD, RW_HEAD), F32)
        for s in range(NS):
            for p in range(RW_PAIRS):
                rwS_ref[s, p] = jnp.concatenate(
                    [jnp.concatenate([rw0_ref[s, 2 * p], zero], axis=1),
                     jnp.concatenate([zero, rw0_ref[s, 2 * p + 1]], axis=1)], axis=0)
        sh_ref[...] = sh0_ref[...]

    row = lax.broadcasted_iota(jnp.int32, (C, LANES), 0)
    lane = lax.broadcasted_iota(jnp.int32, (C, LANES), 1)
    ri = lax.broadcasted_iota(jnp.int32, (C, C), 0)
    ci = lax.broadcasted_iota(jnp.int32, (C, C), 1)
    tril_bf = (ci <= ri).astype(F32).astype(BF16)
    levels = [b for b in (8, 16, 32) if 2 * b <= C]
    level_sign = {b: jnp.where((row & b) != 0, 1.0, -1.0) for b in levels}
    level_qk = {b: functools.partial(jnp.where, (row & b) != 0) for b in levels}
    level_keep = {b: ((ri & b) != 0) & ((ci & b) == 0) & ((ri & (-2 * b)) == (ci & (-2 * b))) for b in levels}
    diag_keep = ((lane >> 3) == (row >> 3)) & (lane <= row)
    head0 = lane < RW_HEAD
    blockdiag = ((lax.broadcasted_iota(jnp.int32, (LANES, LANES), 0) < RW_HEAD)
                 == (lax.broadcasted_iota(jnp.int32, (LANES, LANES), 1) < RW_HEAD))
    ones_bf = blockdiag.astype(F32).astype(BF16)

    def group_sums(xs):
        s = jnp.dot(_bf(jnp.concatenate(xs, axis=0)), ones_bf, preferred_element_type=F32)
        return [s[i * C:(i + 1) * C] for i in range(len(xs))]
    C2 = 2 * C
    aligned = C2 % LANES == 0
    ri2 = lax.broadcasted_iota(jnp.int32, (C, C2), 0)
    ci2 = lax.broadcasted_iota(jnp.int32, (C, C2), 1)
    cj2 = jnp.where(ci2 >= C, ci2 - C, ci2)
    smask = cj2 < ri2
    imask = cj2 <= ri2
    first = ci2 < C
    eye2 = (lax.broadcasted_iota(jnp.int32, (C2, C2), 0)
            == lax.broadcasted_iota(jnp.int32, (C2, C2), 1)).astype(F32)
    n_dbl = C.bit_length() - 1
    half = HG_HEADS // 2
    pairs = range(RW_PAIRS)
    sls = [slice(p * LANES, (p + 1) * LANES) for p in pairs]

    h0 = head0.astype(F32).astype(BF16)
    h1 = (~head0).astype(F32).astype(BF16)

    def by_head(x):
        xb = _bf(x)
        return jnp.concatenate([xb * h0, xb * h1], axis=0)

    def two_rows(ref, base, hs):
        return jnp.concatenate([jnp.broadcast_to(ref[base:base + 1, hs], (SUB, LANES)),
                                jnp.broadcast_to(ref[base + SUB:base + SUB + 1, hs], (SUB, LANES))], axis=0)

    def token_shift(s, blk):
        p = proj_ref[HG_BLK + blk, s]
        prev = pltpu.roll(p, 1, axis=0)
        prev = jnp.concatenate([jnp.where(row[:SUB] == 0, sh_ref[s, blk], prev[:SUB]), prev[SUB:]], axis=0)
        return p + (prev - p) * mu_ref[blk]

    xl = jnp.concatenate([token_shift(s, RW_BLK - 1) for s in range(NS)], axis=0)
    xl = jnp.where(jnp.concatenate([lane] * NS, axis=0) < RW_HEAD, jnp.tanh(xl), xl)
    lora = jnp.dot(_bf(xl), w2a_ref[...], preferred_element_type=F32)
    for s in range(NS):
        lora_all[s] = lora[s * C:(s + 1) * C]

    def program(s):
        proj = proj_ref.at[:, s]
        out = out_ref.at[s]
        hgS, rwS, sh = hgS_ref.at[s], rwS_ref.at[s], sh_ref.at[s]
        q_s, k_s, g_s, lk_s, gk_s = q_all.at[s], k_all.at[s], g_all.at[s], lk_all.at[s], gk_all.at[s]
        lw_s, gam_s, cum_s, lora_s = lw_all.at[s], gam_all.at[s], cum_all.at[s], lora_all.at[s]

        def hg_gates(_):
            for h in range(HG_HEADS):
                hs = slice(h * LANES, (h + 1) * LANES)
                pq = proj[h]
                lb = lb_ref[:, hs]
                forget = lb + (1.0 - lb) * _sigmoid(proj[HG_HEADS + h])
                k = 1.0 - forget
                q_s[:, hs] = pq * _sigmoid(pq)
                k_s[:, hs] = k
                lk_s[:, hs] = jnp.log2(k)
                g_s[:, hs] = jnp.log2(forget)
            cum_s[...] = _dot_exact_lhs(tril_bf, g_s[...])

        scores_off = [None] * HG_HEADS
        diag_lhs = [None] * HG_HEADS
        diags = {}

        def hg_scores(h):
            hs = slice(h * LANES, (h + 1) * LANES)
            q = q_s[:, hs]
            k = k_s[:, hs]
            lk = lk_s[:, hs]
            G = cum_s[0:C, hs]
            gk_s[:, hs] = G - lk
            scores = jnp.zeros((C, C), F32)
            for b in levels:
                gref = jnp.concatenate(
                    [jnp.broadcast_to(cum_s[2 * b * blk + b - 1:2 * b * blk + b, hs], (2 * b, LANES))
                     for blk in range(C // (2 * b))], axis=0)
                me = _bf(level_qk[b](q, k) * jnp.exp2((G - gref) * level_sign[b]))
                sc = lax.dot_general(me, me, (((1,), (1,)), ((), ())), preferred_element_type=F32)
                scores = jnp.where(level_keep[b], sc, scores)
            scores_off[h] = scores

            units = []
            for u in range(C // (2 * SUB)):
                r0 = u * 2 * SUB
                G16 = G[r0:r0 + 2 * SUB]
                q16 = _bf(q[r0:r0 + 2 * SUB])
                cols = []
                for j in range(SUB):
                    x = jnp.minimum(G16 - two_rows(gk_s, r0 + j, hs), two_rows(lk_s, r0 + j, hs))
                    cols.append(q16 * _bf(jnp.exp2(x)))
                units.append(jnp.concatenate(cols, axis=1))
            diag_lhs[h] = jnp.concatenate(units, axis=0)

        def hg_diag(part):
            lhs = jnp.concatenate(diag_lhs[part * half:(part + 1) * half], axis=0)
            diags[part] = jnp.dot(lhs, wsel_ref[...], preferred_element_type=F32)

        def hg_finish(h):
            hs = slice(h * LANES, (h + 1) * LANES)
            diag_h = diags[h // half][(h % half) * C:(h % half + 1) * C]
            G = cum_s[0:C, hs]
            v = proj[2 * HG_HEADS + h]
            pg = proj[3 * HG_HEADS + h]
            St = hgS[h]
            scores = scores_off[h] + jnp.where(diag_keep, diag_h, 0.0)[:, :C]
            o = _dot_nt(q_s[:, hs] * jnp.exp2(G), St) + _dot(scores, v)

            Gl = G[C - 1:C, :]
            hgS[h] = St * jnp.exp2(Gl) + _dot_tn(v, jnp.exp2(Gl - gk_s[:, hs]))

            o = o * lax.rsqrt(jnp.mean(o * o, axis=-1, keepdims=True) + NORM_EPS)
            o = o * hgn_ref[:, hs] * (pg * _sigmoid(pg))
            out[:, hs] = o.astype(out.dtype)

        S_, F_, D_ = hg_scores, hg_finish, hg_diag
        fillers = [(hg_gates, 0), (S_, 0), (S_, 1), (S_, 2), (S_, 3), (D_, 0), (S_, 4), (S_, 5), (F_, 0), (F_, 1),
                   (S_, 6), (F_, 2), (F_, 3), (S_, 7), (D_, 1), (F_, 4), (F_, 5), (F_, 6), (F_, 7)]

        def fill(n):
            for _ in range(min(n, len(fillers))):
                fn, arg = fillers.pop(0)
                fn(arg)

        shifted = functools.partial(token_shift, s)
        st = [dict() for _ in pairs]

        for p, d in zip(pairs, st):
            sl = sls[p]
            d["r"] = shifted(p)
            d["xk"] = shifted(RW_PAIRS + p)
            d["v"] = shifted(2 * RW_PAIRS + p)
            w = w0_ref[:, sl] + lora_s[:, sl]
            lw_s[:, sl] = (-RW_DECAY_SCALE * LOG2E) * _sigmoid(w)
            d["a"] = _sigmoid(a0_ref[:, sl] + lora_s[:, RW_WIDTH + p * LANES:RW_WIDTH + (p + 1) * LANES])
            d["kkr"] = d["xk"] * kk_ref[:, sl]
            d["k"] = d["xk"] * (1.0 + (d["a"] - 1.0) * ka_ref[:, sl])
        yield
        for d, ss in zip(st, group_sums([d["kkr"] * d["kkr"] for d in st])):
            d["kk"] = d["kkr"] * lax.rsqrt(jnp.maximum(ss, KK_EPS * KK_EPS))
            d["b"] = d["kk"] * d["a"]
        for d, bonus in zip(st, group_sums([d["r"] * d["k"] * rk_ref[:, sls[p]] for p, d in zip(pairs, st)])):
            d["bonus"] = bonus

        gam_s[...] = _dot_exact_lhs(tril_bf, lw_s[...])
        yield

        for p, d in zip(pairs, st):
            sl = sls[p]
            gam = gam_s[:, sl]
            eg = jnp.exp2(gam)
            eng = 1.0 / eg
            at = -d["kk"] * jnp.exp2(gam - lw_s[:, sl])
            rt = d["r"] * eg
            bt = d["b"] * eng
            kt = d["k"] * eng
            dl = eg[C - 1:C, :] * eng
            d["bk_dl"] = _bf(jnp.concatenate([d["b"] * dl, d["k"] * dl], axis=0))
            d["eg_last"] = eg[C - 1:C, :]
            d["S"] = rwS[p]
            at_rt = _bf(jnp.concatenate([at, rt], axis=0))
            bt, kt = _bf(bt), _bf(kt)
            rhs0 = jnp.concatenate([bt * h0, kt * h0], axis=0)
            rhs1 = jnp.concatenate([kt * h1, bt * h1], axis=0)
            if aligned:
                m = _dot_nt(at_rt, jnp.concatenate([rhs0, rhs1, _bf(d["S"])], axis=0))
                m0, m1, d["wst"] = m[:, :C2], m[:, C2:2 * C2], m[:, 2 * C2:]
            else:
                m0, m1, d["wst"] = _dot_nt(at_rt, rhs0), _dot_nt(at_rt, rhs1), _dot_nt(at_rt, d["S"])
            aa0 = jnp.where(smask, m0[:C], 0.0)
            aa1 = jnp.where(smask, m1[:C], 0.0)
            d["ak"] = _bf(jnp.where(first, aa1, aa0))
            d["ar0"] = _bf(jnp.where(imask, m0[C:], 0.0))
            d["ar1"] = _bf(jnp.where(imask, m1[C:], 0.0))
            pw = jnp.concatenate([jnp.where(first, aa0, 0.0), jnp.where(first, 0.0, aa1)], axis=0)
            d["t"] = eye2 + pw
            d["pw"] = _bf(pw)
        fill(1)
        yield

        for d in st:
            d["pw"] = _bf(_dot(d["pw"], d["pw"]))
        fill(1)
        yield
        for _ in range(n_dbl - 2):
            for d in st:
                if aligned:
                    res = _dot(d["pw"], jnp.concatenate([d["pw"], _bf(d["t"])], axis=1))
                    d["pw"], d["t"] = _bf(res[:, :C2]), d["t"] + res[:, C2:]
                else:
                    d["pw"], d["t"] = _bf(_dot(d["pw"], d["pw"])), d["t"] + _dot(d["pw"], d["t"])
            fill(1)
            yield

        for d in st:
            vb = _bf(d["v"])
            d["v0"], d["v1"] = vb * h0, vb * h1
            akv = _dot(d["ak"], jnp.concatenate([d["v1"], d["v0"]], axis=0))
            d["rhs2"] = by_head(d["wst"][:C] + akv)
        fill(1)
        yield
        for d in st:
            d["w"] = _dot(d["t"][:C] + d["t"][C:], d["rhs2"])
        fill(3)
        yield
        for d in st:
            d["u"] = d["w"] + _dot(d["pw"][:C] + d["pw"][C:], by_head(d["w"]))
        fill(3)
        yield

        for p, d in zip(pairs, st):
            u, v = d["u"], d["v"]
            ub = _bf(u)
            u0, u1 = ub * h0, ub * h1
            if aligned:
                yuv = _dot(jnp.concatenate([d["ar0"], d["ar1"]], axis=1),
                           jnp.concatenate([u0, d["v0"], d["v1"], u1], axis=0))
            else:
                yuv = (_dot(d["ar0"], jnp.concatenate([u0, d["v0"]], axis=0))
                       + _dot(d["ar1"], jnp.concatenate([d["v1"], u1], axis=0)))
            d["y"] = d["wst"][C:] + yuv
            upd = _dot_tn(jnp.concatenate([ub, _bf(v)], axis=0), d["bk_dl"])
            rwS[p] = d["S"] * d["eg_last"] + jnp.where(blockdiag, upd, 0.0)
        fill(1)
        yield

        for d, m in zip(st, group_sums([d["y"] for d in st])):
            d["yc"] = d["y"] - m * (1.0 / RW_HEAD)
        fill(1)
        yield
        for d, var in zip(st, group_sums([d["yc"] * d["yc"] for d in st])):
            d["var"] = var
        fill(2)
        yield
        for p, d in zip(pairs, st):
            sl = sls[p]
            yn = d["yc"] * lax.rsqrt(d["var"] * (1.0 / RW_HEAD) + RW_GN_EPS) * lnw_ref[:, sl] + lnb_ref[:, sl]
            xg = shifted(3 * RW_PAIRS + p)
            yo = (yn + d["bonus"] * d["v"]) * (xg * _sigmoid(xg))
            out[:, HG_WIDTH + p * LANES:HG_WIDTH + (p + 1) * LANES] = yo.astype(out.dtype)
        fill(len(fillers))

        for blk in range(RW_BLK):
            sh[blk] = proj[HG_BLK + blk, C - 1:C, :]

    todo = [(s * stagger, program(s)) for s in range(NS)]
    rnd = 0
    while todo:
        todo = [(t0, g) for t0, g in todo if rnd < t0 or next(g, StopIteration) is not StopIteration]
        rnd += 1

    @pl.when(pl.program_id(1) == pl.num_programs(1) - 1)
    def _():
        for s in range(NS):
            for h in range(HG_HEADS):
                hgS_ref[s, h] = hgS_ref[s, h].T
            for p in range(RW_PAIRS):
                S = rwS_ref[s, p]
                rw_out_ref[s, 2 * p] = S[:RW_HEAD, :RW_HEAD]
                rw_out_ref[s, 2 * p + 1] = S[RW_HEAD:, RW_HEAD:]


def _mixer(proj, hg0, rw0, sh0, params, B, T, C):
    nc = T // C
    NS = STREAMS_PER_STEP if B % STREAMS_PER_STEP == 0 else 1
    full = lambda shape: pl.BlockSpec(shape, lambda b, c: (0,) * len(shape))
    per_b = lambda shape: pl.BlockSpec((NS,) + shape, lambda b, c: (b,) + (0,) * len(shape))
    lb, hgn, mu, w0, a0, w2a, kk, ka, rk, lnw, lnb = params
    wsel = ((jnp.arange(SUB * LANES)[:, None] // LANES) == (jnp.arange(LANES)[None, :] % SUB)).astype(BF16)
    in_specs = [
        pl.BlockSpec((NBLK, NS, C, LANES), lambda b, c: (0, b, c, 0)),
        per_b((HG_HEADS, LANES, LANES)),
        per_b((2 * RW_PAIRS, RW_HEAD, RW_HEAD)),
        per_b((RW_BLK, 1, LANES)),
        full((1, HG_WIDTH)), full((1, HG_WIDTH)), full((RW_BLK, 1, LANES)),
        full((1, RW_WIDTH)), full((1, RW_WIDTH)), full((LANES, 2 * RW_WIDTH)),
        full((1, RW_WIDTH)), full((1, RW_WIDTH)), full((1, RW_WIDTH)),
        full((1, RW_WIDTH)), full((1, RW_WIDTH)), full((SUB * LANES, LANES)),
    ]
    out_specs = [
        pl.BlockSpec((NS, C, D_MODEL), lambda b, c: (b, c, 0)),
        per_b((HG_HEADS, LANES, LANES)),
        per_b((2 * RW_PAIRS, RW_HEAD, RW_HEAD)),
        per_b((RW_BLK, 1, LANES)),
    ]
    out_shape = [
        jax.ShapeDtypeStruct((B, T, D_MODEL), BF16),
        jax.ShapeDtypeStruct((B, HG_HEADS, LANES, LANES), F32),
        jax.ShapeDtypeStruct((B, 2 * RW_PAIRS, RW_HEAD, RW_HEAD), F32),
        jax.ShapeDtypeStruct((B, RW_BLK, 1, LANES), F32),
    ]
    scratch = ([pltpu.VMEM((NS, C, HG_WIDTH), F32)] * 8
               + [pltpu.VMEM((NS, C, 2 * RW_WIDTH), F32)]
               + [pltpu.VMEM((NS, RW_PAIRS, LANES, LANES), F32)])
    mixed, hg, rw, sh = pl.pallas_call(
        functools.partial(_mixer_kernel, C=C, NS=NS, stagger=STAGGER),
        grid=(B // NS, nc),
        in_specs=in_specs,
        out_specs=out_specs,
        out_shape=out_shape,
        scratch_shapes=scratch,
        compiler_params=pltpu.CompilerParams(
            dimension_semantics=("arbitrary", "arbitrary"), vmem_limit_bytes=VMEM_LIMIT),
        name="mixer",
    )(proj.reshape(NBLK, B, T, LANES), hg0, rw0, sh0, lb, hgn, mu, w0, a0, w2a, kk, ka, rk, lnw, lnb, wsel)
    return mixed.reshape(B * T, D_MODEL), hg, rw, sh


def _outproj_kernel(m_ref, w_ref, g_ref, x_ref, o_ref):
    z = jnp.dot(m_ref[...], w_ref[...], preferred_element_type=F32)
    ms = jnp.mean(z * z, axis=-1, keepdims=True)
    o_ref[...] = x_ref[...] + z * lax.rsqrt(ms + NORM_EPS) * g_ref[...]


def _outproj(mixed, w_bf, g, x, tm):
    rows = x.shape[0]
    return pl.pallas_call(
        _outproj_kernel,
        grid=(rows // tm,),
        in_specs=[
            pl.BlockSpec((tm, D_MODEL), lambda i: (i, 0)),
            pl.BlockSpec((D_MODEL, D_MODEL), lambda i: (0, 0)),
            pl.BlockSpec((1, D_MODEL), lambda i: (0, 0)),
            pl.BlockSpec((tm, D_MODEL), lambda i: (i, 0)),
        ],
        out_specs=pl.BlockSpec((tm, D_MODEL), lambda i: (i, 0)),
        out_shape=jax.ShapeDtypeStruct((rows, D_MODEL), F32),
        compiler_params=pltpu.CompilerParams(
            dimension_semantics=("arbitrary",), vmem_limit_bytes=VMEM_LIMIT),
        name="outproj",
    )(mixed, w_bf, g, x)


def _pick_tile(rows, cap):
    t = cap
    while rows % t:
        t //= 2
    return t


def _trunk(x, hg0, rw0, sh0, weights, C, need_y=True):
    (norm_pre, w_in_bf, mixer_params, w_out_bf, norm_post) = weights
    B, T, _ = x.shape
    x2 = x.reshape(B * T, D_MODEL)
    proj = _inproj(x2, norm_pre, w_in_bf, _pick_tile(B * T, INPROJ_ROWS))
    mixed, hg, rw, sh = _mixer(proj, hg0, rw0, sh0, mixer_params, B, T, C)
    y = None
    if need_y:
        y = _outproj(mixed, w_out_bf, norm_post, x2, _pick_tile(B * T, OUTPROJ_ROWS)).reshape(B, T, D_MODEL)
    return y, hg, rw, sh


def kernel(x_prompt, x_sample, state_hgrn, state_rwkv, state_shift, meta_tokens, norm_pre, w_in,
           hg_lower_bounds, hg_norm, rw_mu, rw_w0, rw_w2, rw_a0, rw_a2, rw_k_k, rw_k_a, rw_r_k,
           rw_ln_w, rw_ln_b, w_out, norm_post):
    B = x_prompt.shape[0]
    Bs = x_sample.shape[0]
    n_meta = meta_tokens.shape[0]
    l = 0

    lbs = jnp.cumsum(jax.nn.softmax(hg_lower_bounds.astype(F32), axis=0), axis=0)[l].reshape(1, HG_WIDTH)
    z64 = jnp.zeros((RW_HEAD, RW_WIDTH), F32)
    w2a = jnp.concatenate([jnp.concatenate([rw_w2[l], z64], axis=1),
                           jnp.concatenate([z64, rw_a2[l]], axis=1)], axis=0).astype(BF16)
    row = lambda t: t.reshape(1, -1).astype(F32)
    mixer_params = (lbs, row(hg_norm[l]), rw_mu[l].reshape(RW_BLK, 1, LANES), row(rw_w0[l]), row(rw_a0[l]),
                    w2a, row(rw_k_k[l]), row(rw_k_a[l]), row(rw_r_k[l]), row(rw_ln_w[l]), row(rw_ln_b[l]))
    weights = (row(norm_pre[l]), w_in[l].astype(BF16), mixer_params, w_out[l].astype(BF16), row(norm_post[l]))

    zero_hg = jnp.zeros((1, HG_HEADS, LANES, LANES), F32)
    zero_rw = jnp.zeros((1, 2 * RW_PAIRS, RW_HEAD, RW_HEAD), F32)
    zero_sh = jnp.zeros((1, RW_BLK, 1, LANES), F32)
    _, hg_m, rw_m, sh_m = _trunk(meta_tokens[None].astype(F32), zero_hg, zero_rw, zero_sh, weights,
                                 C=n_meta, need_y=False)

    rep = lambda t: jnp.broadcast_to(t, (B,) + t.shape[1:])
    y_p, hg_p, rw_p, sh_p = _trunk(x_prompt, rep(hg_m), rep(rw_m), rep(sh_m), weights, C=CHUNK)

    sh_s0 = state_shift[l].reshape(Bs, RW_BLK, 1, LANES)
    y_s, hg_s, rw_s, sh_s = _trunk(x_sample, state_hgrn[l], state_rwkv[l], sh_s0, weights, C=CHUNK)

    unsh = lambda t: t.reshape(t.shape[0], 1, RW_BLK * LANES)[None]
    return (y_p, y_s,
            hg_p[None], rw_p[None], unsh(sh_p),
            hg_s[None], rw_s[None], unsh(sh_s))
```

```python
import functools

import jax
import jax.numpy as jnp
from jax import lax
from jax.experimental import pallas as pl
from jax.experimental.pallas import tpu as pltpu

F32 = jnp.float32
BF16 = jnp.bfloat16

LANES = 128
SUB = 8
D_MODEL = 2048
HG_HEADS = 8
HG_WIDTH = 1024
RW_PAIRS = 8
RW_HEAD = 64
RW_WIDTH = 1024
P_TOTAL = 8320
NBLK = P_TOTAL // LANES
HG_BLK = 32
RW_BLK = NBLK - HG_BLK
NORM_EPS = 1e-6
RW_GN_EPS = 64e-5
KK_EPS = 1e-12
VMEM_LIMIT = 56 * 1024 * 1024
LOG2E = 1.4426950408889634
RW_DECAY_SCALE = 0.6065306597126334
K_FLOOR = 2.0 ** -126
CHUNK = 64
STREAMS_PER_STEP = 2
STAGGER = 1
INPROJ_ROWS = 1024
INPROJ_COL_BLOCKS = 13
OUTPROJ_ROWS = 512


def _bf(x):
    return x.astype(BF16)


def _dot(a, b):
    return jnp.dot(_bf(a), _bf(b), preferred_element_type=F32)


def _dot_nt(a, b):
    return lax.dot_general(_bf(a), _bf(b), (((1,), (1,)), ((), ())), preferred_element_type=F32)


def _dot_tn(a, b):
    return lax.dot_general(_bf(a), _bf(b), (((0,), (0,)), ((), ())), preferred_element_type=F32)


def _dot_exact_lhs(l_bf, x):
    hi = _bf(x)
    lo = _bf(x - hi.astype(F32))
    return jnp.dot(jnp.concatenate([l_bf, l_bf], axis=1), jnp.concatenate([hi, lo], axis=0),
                   preferred_element_type=F32)


def _sigmoid(x):
    return 1.0 / (1.0 + jnp.exp2(x * (-LOG2E)))


def _inproj_kernel(x_ref, g_ref, w_ref, o_ref, u_ref, *, nb):
    @pl.when(pl.program_id(1) == 0)
    def _():
        x = x_ref[...]
        ms = jnp.mean(x * x, axis=-1, keepdims=True)
        u_ref[...] = _bf(x * lax.rsqrt(ms + NORM_EPS) * g_ref[...])

    acc = jnp.dot(u_ref[...], w_ref[...], preferred_element_type=F32)
    for cb in range(nb):
        o_ref[cb] = acc[:, cb * LANES:(cb + 1) * LANES]


def _inproj(x, g, w_bf, tm):
    rows = x.shape[0]
    nb = INPROJ_COL_BLOCKS
    tn = nb * LANES
    return pl.pallas_call(
        functools.partial(_inproj_kernel, nb=nb),
        grid=(rows // tm, NBLK // nb),
        in_specs=[
            pl.BlockSpec((tm, D_MODEL), lambda i, j: (i, 0)),
            pl.BlockSpec((1, D_MODEL), lambda i, j: (0, 0)),
            pl.BlockSpec((D_MODEL, tn), lambda i, j: (0, j)),
        ],
        out_specs=pl.BlockSpec((nb, tm, LANES), lambda i, j: (j, i, 0)),
        out_shape=jax.ShapeDtypeStruct((NBLK, rows, LANES), F32),
        scratch_shapes=[pltpu.VMEM((tm, D_MODEL), BF16)],
        compiler_params=pltpu.CompilerParams(
            dimension_semantics=("arbitrary", "arbitrary"), vmem_limit_bytes=VMEM_LIMIT),
        name="inproj",
    )(x, g, w_bf)


def _mixer_kernel(proj_ref, hg0_ref, rw0_ref, sh0_ref, lb_ref, hgn_ref, mu_ref, w0_ref, a0_ref,
                  w2a_ref, kk_ref, ka_ref, rk_ref, lnw_ref, lnb_ref, wsel_ref,
                  out_ref, hgS_ref, rw_out_ref, sh_ref,
                  q_all, k_all, g_all, lk_all, gk_all, lw_all, gam_all, cum_all, lora_all, rwS_ref,
                  *, C, NS, stagger):
    @pl.when(pl.program_id(1) == 0)
    def _():
        for s in range(NS):
            for h in range(HG_HEADS):
                hgS_ref[s, h] = hg0_ref[s, h].T
        zero = jnp.zeros((RW_HEAD, RW_HEAD), F32)
        for s in range(NS):
            for p in range(RW_PAIRS):
                rwS_ref[s, p] = jnp.concatenate(
                    [jnp.concatenate([rw0_ref[s, 2 * p], zero], axis=1),
                     jnp.concatenate([zero, rw0_ref[s, 2 * p + 1]], axis=1)], axis=0)
        sh_ref[...] = sh0_ref[...]

    row = lax.broadcasted_iota(jnp.int32, (C, LANES), 0)
    lane = lax.broadcasted_iota(jnp.int32, (C, LANES), 1)
    ri = lax.broadcasted_iota(jnp.int32, (C, C), 0)
    ci = lax.broadcasted_iota(jnp.int32, (C, C), 1)
    tril_bf = (ci <= ri).astype(F32).astype(BF16)
    levels = [b for b in (8, 16, 32) if 2 * b <= C]
    level_sign = {b: jnp.where((row & b) != 0, 1.0, -1.0) for b in levels}
    level_qk = {b: functools.partial(jnp.where, (row & b) != 0) for b in levels}
    level_keep = {b: ((ri & b) != 0) & ((ci & b) == 0) & ((ri & (-2 * b)) == (ci & (-2 * b))) for b in levels}
    diag_keep = ((lane >> 3) == (row >> 3)) & (lane <= row)
    head0 = lane < RW_HEAD
    blockdiag = ((lax.broadcasted_iota(jnp.int32, (LANES, LANES), 0) < RW_HEAD)
                 == (lax.broadcasted_iota(jnp.int32, (LANES, LANES), 1) < RW_HEAD))
    ones_bf = blockdiag.astype(F32).astype(BF16)

    def group_sums(xs):
        s = jnp.dot(_bf(jnp.concatenate(xs, axis=0)), ones_bf, preferred_element_type=F32)
        return [s[i * C:(i + 1) * C] for i in range(len(xs))]
    C2 = 2 * C
    aligned = C2 % LANES == 0
    ri2 = lax.broadcasted_iota(jnp.int32, (C, C2), 0)
    ci2 = lax.broadcasted_iota(jnp.int32, (C, C2), 1)
    cj2 = jnp.where(ci2 >= C, ci2 - C, ci2)
    smask = cj2 < ri2
    imask = cj2 <= ri2
    first = ci2 < C
    eye2 = (lax.broadcasted_iota(jnp.int32, (C2, C2), 0)
            == lax.broadcasted_iota(jnp.int32, (C2, C2), 1)).astype(F32)
    n_dbl = C.bit_length() - 1
    half = HG_HEADS // 2
    pairs = range(RW_PAIRS)
    sls = [slice(p * LANES, (p + 1) * LANES) for p in pairs]

    h0 = head0.astype(F32).astype(BF16)
    h1 = (~head0).astype(F32).astype(BF16)

    def by_head(x):
        xb = _bf(x)
        return jnp.concatenate([xb * h0, xb * h1], axis=0)

    def two_rows(ref, base, hs):
        return jnp.concatenate([jnp.broadcast_to(ref[base:base + 1, hs], (SUB, LANES)),
                                jnp.broadcast_to(ref[base + SUB:base + SUB + 1, hs], (SUB, LANES))], axis=0)

    def token_shift(s, blk):
        p = proj_ref[HG_BLK + blk, s]
        prev = pltpu.roll(p, 1, axis=0)
        prev = jnp.concatenate([jnp.where(row[:SUB] == 0, sh_ref[s, blk], prev[:SUB]), prev[SUB:]], axis=0)
        return p + (prev - p) * mu_ref[blk]

    xl = jnp.concatenate([token_shift(s, RW_BLK - 1) for s in range(NS)], axis=0)
    xl = jnp.where(jnp.concatenate([lane] * NS, axis=0) < RW_HEAD, jnp.tanh(xl), xl)
    lora = jnp.dot(_bf(xl), w2a_ref[...], preferred_element_type=F32)
    for s in range(NS):
        lora_all[s] = lora[s * C:(s + 1) * C]

    def program(s):
        proj = proj_ref.at[:, s]
        out = out_ref.at[s]
        hgS, rwS, sh = hgS_ref.at[s], rwS_ref.at[s], sh_ref.at[s]
        q_s, k_s, g_s, lk_s, gk_s = q_all.at[s], k_all.at[s], g_all.at[s], lk_all.at[s], gk_all.at[s]
        lw_s, gam_s, cum_s, lora_s = lw_all.at[s], gam_all.at[s], cum_all.at[s], lora_all.at[s]

        def hg_gates(_):
            for h in range(HG_HEADS):
                hs = slice(h * LANES, (h + 1) * LANES)
                pq = proj[h]
                lb = lb_ref[:, hs]
                forget = lb + (1.0 - lb) * _sigmoid(proj[HG_HEADS + h])
                k = 1.0 - forget
                q_s[:, hs] = pq * _sigmoid(pq)
                k_s[:, hs] = k
                lk_s[:, hs] = jnp.log2(jnp.maximum(k, K_FLOOR))
                g_s[:, hs] = jnp.log2(forget)
            cum_s[...] = _dot_exact_lhs(tril_bf, g_s[...])

        scores_off = [None] * HG_HEADS
        diag_lhs = [None] * HG_HEADS
        diags = {}

        def hg_scores(h):
            hs = slice(h * LANES, (h + 1) * LANES)
            q = q_s[:, hs]
            k = k_s[:, hs]
            lk = lk_s[:, hs]
            G = cum_s[0:C, hs]
            gk_s[:, hs] = G - lk
            scores = jnp.zeros((C, C), F32)
            for b in levels:
                gref = jnp.concatenate(
                    [jnp.broadcast_to(cum_s[2 * b * blk + b - 1:2 * b * blk + b, hs], (2 * b, LANES))
                     for blk in range(C // (2 * b))], axis=0)
                me = _bf(level_qk[b](q, k) * jnp.exp2((G - gref) * level_sign[b]))
                sc = lax.dot_general(me, me, (((1,), (1,)), ((), ())), preferred_element_type=F32)
                scores = jnp.where(level_keep[b], sc, scores)
            scores_off[h] = scores

            units = []
            for u in range(C // (2 * SUB)):
                r0 = u * 2 * SUB
                G16 = G[r0:r0 + 2 * SUB]
                q16 = _bf(q[r0:r0 + 2 * SUB])
                cols = []
                for j in range(SUB):
                    x = jnp.minimum(G16 - two_rows(gk_s, r0 + j, hs), two_rows(lk_s, r0 + j, hs))
                    cols.append(q16 * _bf(jnp.exp2(x)))
                units.append(jnp.concatenate(cols, axis=1))
            diag_lhs[h] = jnp.concatenate(units, axis=0)

        def hg_diag(part):
            lhs = jnp.concatenate(diag_lhs[part * half:(part + 1) * half], axis=0)
            diags[part] = jnp.dot(lhs, wsel_ref[...], preferred_element_type=F32)

        def hg_finish(h):
            hs = slice(h * LANES, (h + 1) * LANES)
            diag_h = diags[h // half][(h % half) * C:(h % half + 1) * C]
            G = cum_s[0:C, hs]
            v = proj[2 * HG_HEADS + h]
            pg = proj[3 * HG_HEADS + h]
            St = hgS[h]
            scores = scores_off[h] + jnp.where(diag_keep, diag_h, 0.0)[:, :C]
            o = _dot_nt(q_s[:, hs] * jnp.exp2(G), St) + _dot(scores, v)

            Gl = G[C - 1:C, :]
            hgS[h] = St * jnp.exp2(Gl) + _dot_tn(v, jnp.exp2(Gl - gk_s[:, hs]))

            o = o * lax.rsqrt(jnp.mean(o * o, axis=-1, keepdims=True) + NORM_EPS)
            o = o * hgn_ref[:, hs] * (pg * _sigmoid(pg))
            out[:, hs] = o.astype(out.dtype)

        S_, F_, D_ = hg_scores, hg_finish, hg_diag
        fillers = [(hg_gates, 0), (S_, 0), (S_, 1), (S_, 2), (S_, 3), (D_, 0), (S_, 4), (S_, 5), (F_, 0), (F_, 1),
                   (S_, 6), (F_, 2), (F_, 3), (S_, 7), (D_, 1), (F_, 4), (F_, 5), (F_, 6), (F_, 7)]

        def fill(n):
            for _ in range(min(n, len(fillers))):
                fn, arg = fillers.pop(0)
                fn(arg)

        shifted = functools.partial(token_shift, s)
        st = [dict() for _ in pairs]

        for p, d in zip(pairs, st):
            sl = sls[p]
            d["r"] = shifted(p)
            d["xk"] = shifted(RW_PAIRS + p)
            d["v"] = shifted(2 * RW_PAIRS + p)
            w = w0_ref[:, sl] + lora_s[:, sl]
            lw_s[:, sl] = (-RW_DECAY_SCALE * LOG2E) * _sigmoid(w)
            d["a"] = _sigmoid(a0_ref[:, sl] + lora_s[:, RW_WIDTH + p * LANES:RW_WIDTH + (p + 1) * LANES])
            d["kkr"] = d["xk"] * kk_ref[:, sl]
            d["k"] = d["xk"] * (1.0 + (d["a"] - 1.0) * ka_ref[:, sl])
        yield
        for d, ss in zip(st, group_sums([d["kkr"] * d["kkr"] for d in st])):
            d["kk"] = d["kkr"] * lax.rsqrt(jnp.maximum(ss, KK_EPS * KK_EPS))
            d["b"] = d["kk"] * d["a"]
        for d, bonus in zip(st, group_sums([d["r"] * d["k"] * rk_ref[:, sls[p]] for p, d in zip(pairs, st)])):
            d["bonus"] = bonus

        gam_s[...] = _dot_exact_lhs(tril_bf, lw_s[...])
        yield

        for p, d in zip(pairs, st):
            sl = sls[p]
            gam = gam_s[:, sl]
            eg = jnp.exp2(gam)
            eng = 1.0 / eg
            at = -d["kk"] * jnp.exp2(gam - lw_s[:, sl])
            rt = d["r"] * eg
            bt = d["b"] * eng
            kt = d["k"] * eng
            dl = eg[C - 1:C, :] * eng
            d["bk_dl"] = _bf(jnp.concatenate([d["b"] * dl, d["k"] * dl], axis=0))
            d["eg_last"] = eg[C - 1:C, :]
            d["S"] = rwS[p]
            at_rt = _bf(jnp.concatenate([at, rt], axis=0))
            bt, kt = _bf(bt), _bf(kt)
            rhs0 = jnp.concatenate([bt * h0, kt * h0], axis=0)
            rhs1 = jnp.concatenate([kt * h1, bt * h1], axis=0)
            if aligned:
                m = _dot_nt(at_rt, jnp.concatenate([rhs0, rhs1, _bf(d["S"])], axis=0))
                m0, m1, d["wst"] = m[:, :C2], m[:, C2:2 * C2], m[:, 2 * C2:]
            else:
                m0, m1, d["wst"] = _dot_nt(at_rt, rhs0), _dot_nt(at_rt, rhs1), _dot_nt(at_rt, d["S"])
            aa0 = jnp.where(smask, m0[:C], 0.0)
            aa1 = jnp.where(smask, m1[:C], 0.0)
            d["ak"] = _bf(jnp.where(first, aa1, aa0))
            d["ar0"] = _bf(jnp.where(imask, m0[C:], 0.0))
            d["ar1"] = _bf(jnp.where(imask, m1[C:], 0.0))
            pw = jnp.concatenate([jnp.where(first, aa0, 0.0), jnp.where(first, 0.0, aa1)], axis=0)
            d["t"] = eye2 + pw
            d["pw"] = _bf(pw)
        fill(1)
        yield

        for d in st:
            d["pw"] = _bf(_dot(d["pw"], d["pw"]))
        fill(1)
        yield
        for _ in range(n_dbl - 2):
            for d in st:
                if aligned:
                    res = _dot(d["pw"], jnp.concatenate([d["pw"], _bf(d["t"])], axis=1))
                    d["pw"], d["t"] = _bf(res[:, :C2]), d["t"] + res[:, C2:]
                else:
                    d["pw"], d["t"] = _bf(_dot(d["pw"], d["pw"])), d["t"] + _dot(d["pw"], d["t"])
            fill(1)
            yield

        for d in st:
            vb = _bf(d["v"])
            d["v0"], d["v1"] = vb * h0, vb * h1
            akv = _dot(d["ak"], jnp.concatenate([d["v1"], d["v0"]], axis=0))
            d["rhs2"] = by_head(d["wst"][:C] + akv)
        fill(1)
        yield
        for d in st:
            d["w"] = _dot(d["t"][:C] + d["t"][C:], d["rhs2"])
        fill(3)
        yield
        for d in st:
            d["u"] = d["w"] + _dot(d["pw"][:C] + d["pw"][C:], by_head(d["w"]))
        fill(3)
        yield

        for p, d in zip(pairs, st):
            u, v = d["u"], d["v"]
            ub = _bf(u)
            u0, u1 = ub * h0, ub * h1
            if aligned:
                yuv = _dot(jnp.concatenate([d["ar0"], d["ar1"]], axis=1),
                           jnp.concatenate([u0, d["v0"], d["v1"], u1], axis=0))
            else:
                yuv = (_dot(d["ar0"], jnp.concatenate([u0, d["v0"]], axis=0))
                       + _dot(d["ar1"], jnp.concatenate([d["v1"], u1], axis=0)))
            d["y"] = d["wst"][C:] + yuv
            upd = _dot_tn(jnp.concatenate([ub, _bf(v)], axis=0), d["bk_dl"])
            rwS[p] = d["S"] * d["eg_last"] + jnp.where(blockdiag, upd, 0.0)
        fill(1)
        yield

        for d, m in zip(st, group_sums([d["y"] for d in st])):
            d["yc"] = d["y"] - m * (1.0 / RW_HEAD)
        fill(1)
        yield
        for d, var in zip(st, group_sums([d["yc"] * d["yc"] for d in st])):
            d["var"] = var
        fill(2)
        yield
        for p, d in zip(pairs, st):
            sl = sls[p]
            yn = d["yc"] * lax.rsqrt(d["var"] * (1.0 / RW_HEAD) + RW_GN_EPS) * lnw_ref[:, sl] + lnb_ref[:, sl]
            xg = shifted(3 * RW_PAIRS + p)
            yo = (yn + d["bonus"] * d["v"]) * (xg * _sigmoid(xg))
            out[:, HG_WIDTH + p * LANES:HG_WIDTH + (p + 1) * LANES] = yo.astype(out.dtype)
        fill(len(fillers))

        for blk in range(RW_BLK):
            sh[blk] = proj[HG_BLK + blk, C - 1:C, :]

    todo = [(s * stagger, program(s)) for s in range(NS)]
    rnd = 0
    while todo:
        todo = [(t0, g) for t0, g in todo if rnd < t0 or next(g, StopIteration) is not StopIteration]
        rnd += 1

    @pl.when(pl.program_id(1) == pl.num_programs(1) - 1)
    def _():
        for s in range(NS):
            for h in range(HG_HEADS):
                hgS_ref[s, h] = hgS_ref[s, h].T
            for p in range(RW_PAIRS):
                S = rwS_ref[s, p]
                rw_out_ref[s, 2 * p] = S[:RW_HEAD, :RW_HEAD]
                rw_out_ref[s, 2 * p + 1] = S[RW_HEAD:, RW_HEAD:]


def _mixer(proj, hg0, rw0, sh0, params, B, T, C):
    nc = T // C
    NS = STREAMS_PER_STEP if B % STREAMS_PER_STEP == 0 else 1
    full = lambda shape: pl.BlockSpec(shape, lambda b, c: (0,) * len(shape))
    per_b = lambda shape: pl.BlockSpec((NS,) + shape, lambda b, c: (b,) + (0,) * len(shape))
    lb, hgn, mu, w0, a0, w2a, kk, ka, rk, lnw, lnb = params
    wsel = ((jnp.arange(SUB * LANES)[:, None] // LANES) == (jnp.arange(LANES)[None, :] % SUB)).astype(BF16)
    in_specs = [
        pl.BlockSpec((NBLK, NS, C, LANES), lambda b, c: (0, b, c, 0)),
        per_b((HG_HEADS, LANES, LANES)),
        per_b((2 * RW_PAIRS, RW_HEAD, RW_HEAD)),
        per_b((RW_BLK, 1, LANES)),
        full((1, HG_WIDTH)), full((1, HG_WIDTH)), full((RW_BLK, 1, LANES)),
        full((1, RW_WIDTH)), full((1, RW_WIDTH)), full((LANES, 2 * RW_WIDTH)),
        full((1, RW_WIDTH)), full((1, RW_WIDTH)), full((1, RW_WIDTH)),
        full((1, RW_WIDTH)), full((1, RW_WIDTH)), full((SUB * LANES, LANES)),
    ]
    out_specs = [
        pl.BlockSpec((NS, C, D_MODEL), lambda b, c: (b, c, 0)),
        per_b((HG_HEADS, LANES, LANES)),
        per_b((2 * RW_PAIRS, RW_HEAD, RW_HEAD)),
        per_b((RW_BLK, 1, LANES)),
    ]
    out_shape = [
        jax.ShapeDtypeStruct((B, T, D_MODEL), BF16),
        jax.ShapeDtypeStruct((B, HG_HEADS, LANES, LANES), F32),
        jax.ShapeDtypeStruct((B, 2 * RW_PAIRS, RW_HEAD, RW_HEAD), F32),
        jax.ShapeDtypeStruct((B, RW_BLK, 1, LANES), F32),
    ]
    scratch = ([pltpu.VMEM((NS, C, HG_WIDTH), F32)] * 8
               + [pltpu.VMEM((NS, C, 2 * RW_WIDTH), F32)]
               + [pltpu.VMEM((NS, RW_PAIRS, LANES, LANES), F32)])
    mixed, hg, rw, sh = pl.pallas_call(
        functools.partial(_mixer_kernel, C=C, NS=NS, stagger=STAGGER),
        grid=(B // NS, nc),
        in_specs=in_specs,
        out_specs=out_specs,
        out_shape=out_shape,
        scratch_shapes=scratch,
        compiler_params=pltpu.CompilerParams(
            dimension_semantics=("arbitrary", "arbitrary"), vmem_limit_bytes=VMEM_LIMIT),
        name="mixer",
    )(proj.reshape(NBLK, B, T, LANES), hg0, rw0, sh0, lb, hgn, mu, w0, a0, w2a, kk, ka, rk, lnw, lnb, wsel)
    return mixed.reshape(B * T, D_MODEL), hg, rw, sh


def _outproj_kernel(m_ref, w_ref, g_ref, x_ref, o_ref):
    z = jnp.dot(m_ref[...], w_ref[...], preferred_element_type=F32)
    ms = jnp.mean(z * z, axis=-1, keepdims=True)
    o_ref[...] = x_ref[...] + z * lax.rsqrt(ms + NORM_EPS) * g_ref[...]


def _outproj(mixed, w_bf, g, x, tm):
    rows = x.shape[0]
    return pl.pallas_call(
        _outproj_kernel,
        grid=(rows // tm,),
        in_specs=[
            pl.BlockSpec((tm, D_MODEL), lambda i: (i, 0)),
            pl.BlockSpec((D_MODEL, D_MODEL), lambda i: (0, 0)),
            pl.BlockSpec((1, D_MODEL), lambda i: (0, 0)),
            pl.BlockSpec((tm, D_MODEL), lambda i: (i, 0)),
        ],
        out_specs=pl.BlockSpec((tm, D_MODEL), lambda i: (i, 0)),
        out_shape=jax.ShapeDtypeStruct((rows, D_MODEL), F32),
        compiler_params=pltpu.CompilerParams(
            dimension_semantics=("arbitrary",), vmem_limit_bytes=VMEM_LIMIT),
        name="outproj",
    )(mixed, w_bf, g, x)


def _pick_tile(rows, cap):
    t = cap
    while rows % t:
        t //= 2
    return t


def _trunk(x, hg0, rw0, sh0, weights, C, need_y=True):
    (norm_pre, w_in_bf, mixer_params, w_out_bf, norm_post) = weights
    B, T, _ = x.shape
    x2 = x.reshape(B * T, D_MODEL)
    proj = _inproj(x2, norm_pre, w_in_bf, _pick_tile(B * T, INPROJ_ROWS))
    mixed, hg, rw, sh = _mixer(proj, hg0, rw0, sh0, mixer_params, B, T, C)
    y = None
    if need_y:
        y = _outproj(mixed, w_out_bf, norm_post, x2, _pick_tile(B * T, OUTPROJ_ROWS)).reshape(B, T, D_MODEL)
    return y, hg, rw, sh


def kernel(x_prompt, x_sample, state_hgrn, state_rwkv, state_shift, meta_tokens, norm_pre, w_in,
           hg_lower_bounds, hg_norm, rw_mu, rw_w0, rw_w2, rw_a0, rw_a2, rw_k_k, rw_k_a, rw_r_k,
           rw_ln_w, rw_ln_b, w_out, norm_post):
    B = x_prompt.shape[0]
    Bs = x_sample.shape[0]
    n_meta = meta_tokens.shape[0]
    l = 0

    lbs = jnp.cumsum(jax.nn.softmax(hg_lower_bounds.astype(F32), axis=0), axis=0)[l].reshape(1, HG_WIDTH)
    z64 = jnp.zeros((RW_HEAD, RW_WIDTH), F32)
    w2a = jnp.concatenate([jnp.concatenate([rw_w2[l], z64], axis=1),
                           jnp.concatenate([z64, rw_a2[l]], axis=1)], axis=0).astype(BF16)
    row = lambda t: t.reshape(1, -1).astype(F32)
    mixer_params = (lbs, row(hg_norm[l]), rw_mu[l].reshape(RW_BLK, 1, LANES), row(rw_w0[l]), row(rw_a0[l]),
                    w2a, row(rw_k_k[l]), row(rw_k_a[l]), row(rw_r_k[l]), row(rw_ln_w[l]), row(rw_ln_b[l]))
    weights = (row(norm_pre[l]), w_in[l].astype(BF16), mixer_params, w_out[l].astype(BF16), row(norm_post[l]))

    zero_hg = jnp.zeros((1, HG_HEADS, LANES, LANES), F32)
    zero_rw = jnp.zeros((1, 2 * RW_PAIRS, RW_HEAD, RW_HEAD), F32)
    zero_sh = jnp.zeros((1, RW_BLK, 1, LANES), F32)
    _, hg_m, rw_m, sh_m = _trunk(meta_tokens[None].astype(F32), zero_hg, zero_rw, zero_sh, weights,
                                 C=n_meta, need_y=False)

    rep = lambda t: jnp.broadcast_to(t, (B,) + t.shape[1:])
    y_p, hg_p, rw_p, sh_p = _trunk(x_prompt, rep(hg_m), rep(rw_m), rep(sh_m), weights, C=CHUNK)

    sh_s0 = state_shift[l].reshape(Bs, RW_BLK, 1, LANES)
    y_s, hg_s, rw_s, sh_s = _trunk(x_sample, state_hgrn[l], state_rwkv[l], sh_s0, weights, C=CHUNK)

    unsh = lambda t: t.reshape(t.shape[0], 1, RW_BLK * LANES)[None]
    return (y_p, y_s,
            hg_p[None], rw_p[None], unsh(sh_p),
            hg_s[None], rw_s[None], unsh(sh_s))
```

```python
import functools

import jax
import jax.numpy as jnp
from jax import lax
from jax.experimental import pallas as pl
from jax.experimental.pallas import tpu as pltpu

F32 = jnp.float32
BF16 = jnp.bfloat16

LANES = 128
SUB = 8
D_MODEL = 2048
HG_HEADS = 8
HG_WIDTH = 1024
RW_PAIRS = 8
RW_HEAD = 64
RW_WIDTH = 1024
P_TOTAL = 8320
NBLK = P_TOTAL // LANES
HG_BLK = 32
RW_BLK = NBLK - HG_BLK
NORM_EPS = 1e-6
RW_GN_EPS = 64e-5
KK_EPS = 1e-12
VMEM_LIMIT = 56 * 1024 * 1024
LOG2E = 1.4426950408889634
RW_DECAY_SCALE = 0.6065306597126334
K_FLOOR = 2.0 ** -126
CHUNK = 64
STREAMS_PER_STEP = 2
STAGGER = 1
INPROJ_ROWS = 1024
INPROJ_COL_BLOCKS = 13
OUTPROJ_ROWS = 512


def _bf(x):
    return x.astype(BF16)


def _dot(a, b):
    return jnp.dot(_bf(a), _bf(b), preferred_element_type=F32)


def _dot_nt(a, b):
    return lax.dot_general(_bf(a), _bf(b), (((1,), (1,)), ((), ())), preferred_element_type=F32)


def _dot_tn(a, b):
    return lax.dot_general(_bf(a), _bf(b), (((0,), (0,)), ((), ())), preferred_element_type=F32)


def _dot_exact_lhs(l_bf, x):
    hi = _bf(x)
    lo = _bf(x - hi.astype(F32))
    return jnp.dot(jnp.concatenate([l_bf, l_bf], axis=1), jnp.concatenate([hi, lo], axis=0),
                   preferred_element_type=F32)


def _sigmoid(x):
    return 1.0 / (1.0 + jnp.exp2(x * (-LOG2E)))


def _inproj_kernel(x_ref, g_ref, w_ref, o_ref, u_ref, *, nb):
    @pl.when(pl.program_id(1) == 0)
    def _():
        x = x_ref[...]
        ms = jnp.mean(x * x, axis=-1, keepdims=True)
        u_ref[...] = _bf(x * lax.rsqrt(ms + NORM_EPS) * g_ref[...])

    acc = jnp.dot(u_ref[...], w_ref[...], preferred_element_type=F32)
    for cb in range(nb):
        o_ref[cb] = acc[:, cb * LANES:(cb + 1) * LANES]


def _inproj_cast_kernel(x_ref, g_ref, w_ref, o_ref, wb_ref, u_ref, *, nb):
    @pl.when(pl.program_id(0) == 0)
    def _():
        x = x_ref[...]
        ms = jnp.mean(x * x, axis=-1, keepdims=True)
        u_ref[...] = _bf(x * lax.rsqrt(ms + NORM_EPS) * g_ref[...])

    wb = _bf(w_ref[...])
    wb_ref[...] = wb
    acc = jnp.dot(u_ref[...], wb, preferred_element_type=F32)
    for cb in range(nb):
        o_ref[cb] = acc[:, cb * LANES:(cb + 1) * LANES]


def _inproj_cast(x, g, w_f32):
    rows = x.shape[0]
    nb = INPROJ_COL_BLOCKS
    tn = nb * LANES
    return pl.pallas_call(
        functools.partial(_inproj_cast_kernel, nb=nb),
        grid=(NBLK // nb,),
        in_specs=[
            pl.BlockSpec((rows, D_MODEL), lambda j: (0, 0)),
            pl.BlockSpec((1, D_MODEL), lambda j: (0, 0)),
            pl.BlockSpec((D_MODEL, tn), lambda j: (0, j)),
        ],
        out_specs=[pl.BlockSpec((nb, rows, LANES), lambda j: (j, 0, 0)),
                   pl.BlockSpec((D_MODEL, tn), lambda j: (0, j))],
        out_shape=[jax.ShapeDtypeStruct((NBLK, rows, LANES), F32),
                   jax.ShapeDtypeStruct((D_MODEL, P_TOTAL), BF16)],
        scratch_shapes=[pltpu.VMEM((rows, D_MODEL), BF16)],
        compiler_params=pltpu.CompilerParams(
            dimension_semantics=("arbitrary",), vmem_limit_bytes=VMEM_LIMIT),
        name="inproj_cast",
    )(x, g, w_f32)


def _inproj(x, g, w_bf, tm):
    rows = x.shape[0]
    nb = INPROJ_COL_BLOCKS
    tn = nb * LANES
    return pl.pallas_call(
        functools.partial(_inproj_kernel, nb=nb),
        grid=(rows // tm, NBLK // nb),
        in_specs=[
            pl.BlockSpec((tm, D_MODEL), lambda i, j: (i, 0)),
            pl.BlockSpec((1, D_MODEL), lambda i, j: (0, 0)),
            pl.BlockSpec((D_MODEL, tn), lambda i, j: (0, j)),
        ],
        out_specs=pl.BlockSpec((nb, tm, LANES), lambda i, j: (j, i, 0)),
        out_shape=jax.ShapeDtypeStruct((NBLK, rows, LANES), F32),
        scratch_shapes=[pltpu.VMEM((tm, D_MODEL), BF16)],
        compiler_params=pltpu.CompilerParams(
            dimension_semantics=("arbitrary", "arbitrary"), vmem_limit_bytes=VMEM_LIMIT),
        name="inproj",
    )(x, g, w_bf)


def _mixer_kernel(proj_ref, hg0_ref, rw0_ref, sh0_ref, lb_ref, hgn_ref, mu_ref, w0_ref, a0_ref,
                  w2a_ref, kk_ref, ka_ref, rk_ref, lnw_ref, lnb_ref, wsel_ref,
                  out_ref, hgS_ref, rw_out_ref, sh_ref,
                  q_all, k_all, g_all, lk_all, gk_all, lw_all, gam_all, cum_all, lora_all, rwS_ref,
                  *, C, NS, stagger):
    @pl.when(pl.program_id(1) == 0)
    def _():
        for s in range(NS):
            for h in range(HG_HEADS):
                hgS_ref[s, h] = hg0_ref[s, h].T
        zero = jnp.zeros((RW_HEAD, RW_HEAD), F32)
        for s in range(NS):
            for p in range(RW_PAIRS):
                rwS_ref[s, p] = jnp.concatenate(
                    [jnp.concatenate([rw0_ref[s, 2 * p], zero], axis=1),
                     jnp.concatenate([zero, rw0_ref[s, 2 * p + 1]], axis=1)], axis=0)
        sh_ref[...] = sh0_ref[...]

    row = lax.broadcasted_iota(jnp.int32, (C, LANES), 0)
    lane = lax.broadcasted_iota(jnp.int32, (C, LANES), 1)
    ri = lax.broadcasted_iota(jnp.int32, (C, C), 0)
    ci = lax.broadcasted_iota(jnp.int32, (C, C), 1)
    tril_bf = (ci <= ri).astype(F32).astype(BF16)
    levels = [b for b in (8, 16, 32) if 2 * b <= C]
    level_sign = {b: jnp.where((row & b) != 0, 1.0, -1.0) for b in levels}
    level_qk = {b: functools.partial(jnp.where, (row & b) != 0) for b in levels}
    level_keep = {b: ((ri & b) != 0) & ((ci & b) == 0) & ((ri & (-2 * b)) == (ci & (-2 * b))) for b in levels}
    diag_keep = ((lane >> 3) == (row >> 3)) & (lane <= row)
    head0 = lane < RW_HEAD
    blockdiag = ((lax.broadcasted_iota(jnp.int32, (LANES, LANES), 0) < RW_HEAD)
                 == (lax.broadcasted_iota(jnp.int32, (LANES, LANES), 1) < RW_HEAD))
    ones_bf = blockdiag.astype(F32).astype(BF16)

    def group_sums(xs):
        s = jnp.dot(_bf(jnp.concatenate(xs, axis=0)), ones_bf, preferred_element_type=F32)
        return [s[i * C:(i + 1) * C] for i in range(len(xs))]
    C2 = 2 * C
    aligned = C2 % LANES == 0
    ri2 = lax.broadcasted_iota(jnp.int32, (C, C2), 0)
    ci2 = lax.broadcasted_iota(jnp.int32, (C, C2), 1)
    cj2 = jnp.where(ci2 >= C, ci2 - C, ci2)
    smask = cj2 < ri2
    imask = cj2 <= ri2
    first = ci2 < C
    eye2 = (lax.broadcasted_iota(jnp.int32, (C2, C2), 0)
            == lax.broadcasted_iota(jnp.int32, (C2, C2), 1)).astype(F32)
    n_dbl = C.bit_length() - 1
    half = HG_HEADS // 2
    pairs = range(RW_PAIRS)
    sls = [slice(p * LANES, (p + 1) * LANES) for p in pairs]

    h0 = head0.astype(F32).astype(BF16)
    h1 = (~head0).astype(F32).astype(BF16)

    def by_head(x):
        xb = _bf(x)
        return jnp.concatenate([xb * h0, xb * h1], axis=0)

    def two_rows(ref, base, hs):
        return jnp.concatenate([jnp.broadcast_to(ref[base:base + 1, hs], (SUB, LANES)),
                                jnp.broadcast_to(ref[base + SUB:base + SUB + 1, hs], (SUB, LANES))], axis=0)

    def token_shift(s, blk):
        p = proj_ref[HG_BLK + blk, s]
        prev = pltpu.roll(p, 1, axis=0)
        prev = jnp.concatenate([jnp.where(row[:SUB] == 0, sh_ref[s, blk], prev[:SUB]), prev[SUB:]], axis=0)
        return p + (prev - p) * mu_ref[blk]

    xl = jnp.concatenate([token_shift(s, RW_BLK - 1) for s in range(NS)], axis=0)
    xl = jnp.where(jnp.concatenate([lane] * NS, axis=0) < RW_HEAD, jnp.tanh(xl), xl)
    lora = jnp.dot(_bf(xl), w2a_ref[...], preferred_element_type=F32)
    for s in range(NS):
        lora_all[s] = lora[s * C:(s + 1) * C]

    def program(s):
        proj = proj_ref.at[:, s]
        out = out_ref.at[s]
        hgS, rwS, sh = hgS_ref.at[s], rwS_ref.at[s], sh_ref.at[s]
        q_s, k_s, g_s, lk_s, gk_s = q_all.at[s], k_all.at[s], g_all.at[s], lk_all.at[s], gk_all.at[s]
        lw_s, gam_s, cum_s, lora_s = lw_all.at[s], gam_all.at[s], cum_all.at[s], lora_all.at[s]

        def hg_gates(_):
            for h in range(HG_HEADS):
                hs = slice(h * LANES, (h + 1) * LANES)
                pq = proj[h]
                lb = lb_ref[:, hs]
                forget = lb + (1.0 - lb) * _sigmoid(proj[HG_HEADS + h])
                k = 1.0 - forget
                q_s[:, hs] = pq * _sigmoid(pq)
                k_s[:, hs] = k
                lk_s[:, hs] = jnp.log2(jnp.maximum(k, K_FLOOR))
                g_s[:, hs] = jnp.log2(forget)
            cum_s[...] = _dot_exact_lhs(tril_bf, g_s[...])

        scores_off = [None] * HG_HEADS
        diag_lhs = [None] * HG_HEADS
        diags = {}

        def hg_scores(h):
            hs = slice(h * LANES, (h + 1) * LANES)
            q = q_s[:, hs]
            k = k_s[:, hs]
            lk = lk_s[:, hs]
            G = cum_s[0:C, hs]
            gk_s[:, hs] = G - lk
            scores = jnp.zeros((C, C), F32)
            for b in levels:
                gref = jnp.concatenate(
                    [jnp.broadcast_to(cum_s[2 * b * blk + b - 1:2 * b * blk + b, hs], (2 * b, LANES))
                     for blk in range(C // (2 * b))], axis=0)
                me = _bf(level_qk[b](q, k) * jnp.exp2((G - gref) * level_sign[b]))
                sc = lax.dot_general(me, me, (((1,), (1,)), ((), ())), preferred_element_type=F32)
                scores = jnp.where(level_keep[b], sc, scores)
            scores_off[h] = scores

            units = []
            for u in range(C // (2 * SUB)):
                r0 = u * 2 * SUB
                G16 = G[r0:r0 + 2 * SUB]
                q16 = _bf(q[r0:r0 + 2 * SUB])
                cols = []
                for j in range(SUB):
                    x = jnp.minimum(G16 - two_rows(gk_s, r0 + j, hs), two_rows(lk_s, r0 + j, hs))
                    cols.append(q16 * _bf(jnp.exp2(x)))
                units.append(jnp.concatenate(cols, axis=1))
            diag_lhs[h] = jnp.concatenate(units, axis=0)

        def hg_diag(part):
            lhs = jnp.concatenate(diag_lhs[part * half:(part + 1) * half], axis=0)
            diags[part] = jnp.dot(lhs, wsel_ref[...], preferred_element_type=F32)

        def hg_finish(h):
            hs = slice(h * LANES, (h + 1) * LANES)
            diag_h = diags[h // half][(h % half) * C:(h % half + 1) * C]
            G = cum_s[0:C, hs]
            v = proj[2 * HG_HEADS + h]
            pg = proj[3 * HG_HEADS + h]
            St = hgS[h]
            scores = scores_off[h] + jnp.where(diag_keep, diag_h, 0.0)[:, :C]
            o = _dot_nt(q_s[:, hs] * jnp.exp2(G), St) + _dot(scores, v)

            Gl = G[C - 1:C, :]
            hgS[h] = St * jnp.exp2(Gl) + _dot_tn(v, jnp.exp2(Gl - gk_s[:, hs]))

            o = o * lax.rsqrt(jnp.mean(o * o, axis=-1, keepdims=True) + NORM_EPS)
            o = o * hgn_ref[:, hs] * (pg * _sigmoid(pg))
            out[:, hs] = o.astype(out.dtype)

        S_, F_, D_ = hg_scores, hg_finish, hg_diag
        fillers = [(hg_gates, 0), (S_, 0), (S_, 1), (S_, 2), (S_, 3), (D_, 0), (S_, 4), (S_, 5), (F_, 0), (F_, 1),
                   (S_, 6), (F_, 2), (F_, 3), (S_, 7), (D_, 1), (F_, 4), (F_, 5), (F_, 6), (F_, 7)]

        def fill(n):
            for _ in range(min(n, len(fillers))):
                fn, arg = fillers.pop(0)
                fn(arg)

        shifted = functools.partial(token_shift, s)
        st = [dict() for _ in pairs]

        for p, d in zip(pairs, st):
            sl = sls[p]
            d["r"] = shifted(p)
            d["xk"] = shifted(RW_PAIRS + p)
            d["v"] = shifted(2 * RW_PAIRS + p)
            w = w0_ref[:, sl] + lora_s[:, sl]
            lw_s[:, sl] = (-RW_DECAY_SCALE * LOG2E) * _sigmoid(w)
            d["a"] = _sigmoid(a0_ref[:, sl] + lora_s[:, RW_WIDTH + p * LANES:RW_WIDTH + (p + 1) * LANES])
            d["kkr"] = d["xk"] * kk_ref[:, sl]
            d["k"] = d["xk"] * (1.0 + (d["a"] - 1.0) * ka_ref[:, sl])
        yield
        for d, ss in zip(st, group_sums([d["kkr"] * d["kkr"] for d in st])):
            d["kk"] = d["kkr"] * lax.rsqrt(jnp.maximum(ss, KK_EPS * KK_EPS))
            d["b"] = d["kk"] * d["a"]
        for d, bonus in zip(st, group_sums([d["r"] * d["k"] * rk_ref[:, sls[p]] for p, d in zip(pairs, st)])):
            d["bonus"] = bonus

        gam_s[...] = _dot_exact_lhs(tril_bf, lw_s[...])
        yield

        for p, d in zip(pairs, st):
            sl = sls[p]
            gam = gam_s[:, sl]
            eg = jnp.exp2(gam)
            eng = 1.0 / eg
            at = -d["kk"] * jnp.exp2(gam - lw_s[:, sl])
            rt = d["r"] * eg
            bt = d["b"] * eng
            kt = d["k"] * eng
            dl = eg[C - 1:C, :] * eng
            d["bk_dl"] = _bf(jnp.concatenate([d["b"] * dl, d["k"] * dl], axis=0))
            d["eg_last"] = eg[C - 1:C, :]
            d["S"] = rwS[p]
            at_rt = _bf(jnp.concatenate([at, rt], axis=0))
            bt, kt = _bf(bt), _bf(kt)
            rhs0 = jnp.concatenate([bt * h0, kt * h0], axis=0)
            rhs1 = jnp.concatenate([kt * h1, bt * h1], axis=0)
            if aligned:
                m = _dot_nt(at_rt, jnp.concatenate([rhs0, rhs1, _bf(d["S"])], axis=0))
                m0, m1, d["wst"] = m[:, :C2], m[:, C2:2 * C2], m[:, 2 * C2:]
            else:
                m0, m1, d["wst"] = _dot_nt(at_rt, rhs0), _dot_nt(at_rt, rhs1), _dot_nt(at_rt, d["S"])
            aa0 = jnp.where(smask, m0[:C], 0.0)
            aa1 = jnp.where(smask, m1[:C], 0.0)
            d["ak"] = _bf(jnp.where(first, aa1, aa0))
            d["ar0"] = _bf(jnp.where(imask, m0[C:], 0.0))
            d["ar1"] = _bf(jnp.where(imask, m1[C:], 0.0))
            pw = jnp.concatenate([jnp.where(first, aa0, 0.0), jnp.where(first, 0.0, aa1)], axis=0)
            d["t"] = eye2 + pw
            d["pw"] = _bf(pw)
        fill(1)
        yield

        for d in st:
            d["pw"] = _bf(_dot(d["pw"], d["pw"]))
        fill(1)
        yield
        for _ in range(n_dbl - 2):
            for d in st:
                if aligned:
                    res = _dot(d["pw"], jnp.concatenate([d["pw"], _bf(d["t"])], axis=1))
                    d["pw"], d["t"] = _bf(res[:, :C2]), d["t"] + res[:, C2:]
                else:
                    d["pw"], d["t"] = _bf(_dot(d["pw"], d["pw"])), d["t"] + _dot(d["pw"], d["t"])
            fill(1)
            yield

        for d in st:
            vb = _bf(d["v"])
            d["v0"], d["v1"] = vb * h0, vb * h1
            akv = _dot(d["ak"], jnp.concatenate([d["v1"], d["v0"]], axis=0))
            d["rhs2"] = by_head(d["wst"][:C] + akv)
        fill(1)
        yield
        for d in st:
            d["w"] = _dot(d["t"][:C] + d["t"][C:], d["rhs2"])
        fill(3)
        yield
        for d in st:
            d["u"] = d["w"] + _dot(d["pw"][:C] + d["pw"][C:], by_head(d["w"]))
        fill(3)
        yield

        for p, d in zip(pairs, st):
            u, v = d["u"], d["v"]
            ub = _bf(u)
            u0, u1 = ub * h0, ub * h1
            if aligned:
                yuv = _dot(jnp.concatenate([d["ar0"], d["ar1"]], axis=1),
                           jnp.concatenate([u0, d["v0"], d["v1"], u1], axis=0))
            else:
                yuv = (_dot(d["ar0"], jnp.concatenate([u0, d["v0"]], axis=0))
                       + _dot(d["ar1"], jnp.concatenate([d["v1"], u1], axis=0)))
            d["y"] = d["wst"][C:] + yuv
            upd = _dot_tn(jnp.concatenate([ub, _bf(v)], axis=0), d["bk_dl"])
            rwS[p] = d["S"] * d["eg_last"] + jnp.where(blockdiag, upd, 0.0)
        fill(1)
        yield

        for d, m in zip(st, group_sums([d["y"] for d in st])):
            d["yc"] = d["y"] - m * (1.0 / RW_HEAD)
        fill(1)
        yield
        for d, var in zip(st, group_sums([d["yc"] * d["yc"] for d in st])):
            d["var"] = var
        fill(2)
        yield
        for p, d in zip(pairs, st):
            sl = sls[p]
            yn = d["yc"] * lax.rsqrt(d["var"] * (1.0 / RW_HEAD) + RW_GN_EPS) * lnw_ref[:, sl] + lnb_ref[:, sl]
            xg = shifted(3 * RW_PAIRS + p)
            yo = (yn + d["bonus"] * d["v"]) * (xg * _sigmoid(xg))
            out[:, HG_WIDTH + p * LANES:HG_WIDTH + (p + 1) * LANES] = yo.astype(out.dtype)
        fill(len(fillers))

        for blk in range(RW_BLK):
            sh[blk] = proj[HG_BLK + blk, C - 1:C, :]

    todo = [(s * stagger, program(s)) for s in range(NS)]
    rnd = 0
    while todo:
        todo = [(t0, g) for t0, g in todo if rnd < t0 or next(g, StopIteration) is not StopIteration]
        rnd += 1

    @pl.when(pl.program_id(1) == pl.num_programs(1) - 1)
    def _():
        for s in range(NS):
            for h in range(HG_HEADS):
                hgS_ref[s, h] = hgS_ref[s, h].T
            for p in range(RW_PAIRS):
                S = rwS_ref[s, p]
                rw_out_ref[s, 2 * p] = S[:RW_HEAD, :RW_HEAD]
                rw_out_ref[s, 2 * p + 1] = S[RW_HEAD:, RW_HEAD:]


def _mixer(proj, hg0, rw0, sh0, params, B, T, C):
    nc = T // C
    NS = STREAMS_PER_STEP if B % STREAMS_PER_STEP == 0 else 1
    full = lambda shape: pl.BlockSpec(shape, lambda b, c: (0,) * len(shape))
    per_b = lambda shape: pl.BlockSpec((NS,) + shape, lambda b, c: (b,) + (0,) * len(shape))
    lb, hgn, mu, w0, a0, w2a, kk, ka, rk, lnw, lnb = params
    wsel = ((jnp.arange(SUB * LANES)[:, None] // LANES) == (jnp.arange(LANES)[None, :] % SUB)).astype(BF16)
    in_specs = [
        pl.BlockSpec((NBLK, NS, C, LANES), lambda b, c: (0, b, c, 0)),
        per_b((HG_HEADS, LANES, LANES)),
        per_b((2 * RW_PAIRS, RW_HEAD, RW_HEAD)),
        per_b((RW_BLK, 1, LANES)),
        full((1, HG_WIDTH)), full((1, HG_WIDTH)), full((RW_BLK, 1, LANES)),
        full((1, RW_WIDTH)), full((1, RW_WIDTH)), full((LANES, 2 * RW_WIDTH)),
        full((1, RW_WIDTH)), full((1, RW_WIDTH)), full((1, RW_WIDTH)),
        full((1, RW_WIDTH)), full((1, RW_WIDTH)), full((SUB * LANES, LANES)),
    ]
    out_specs = [
        pl.BlockSpec((NS, C, D_MODEL), lambda b, c: (b, c, 0)),
        per_b((HG_HEADS, LANES, LANES)),
        per_b((2 * RW_PAIRS, RW_HEAD, RW_HEAD)),
        per_b((RW_BLK, 1, LANES)),
    ]
    out_shape = [
        jax.ShapeDtypeStruct((B, T, D_MODEL), BF16),
        jax.ShapeDtypeStruct((B, HG_HEADS, LANES, LANES), F32),
        jax.ShapeDtypeStruct((B, 2 * RW_PAIRS, RW_HEAD, RW_HEAD), F32),
        jax.ShapeDtypeStruct((B, RW_BLK, 1, LANES), F32),
    ]
    scratch = ([pltpu.VMEM((NS, C, HG_WIDTH), F32)] * 8
               + [pltpu.VMEM((NS, C, 2 * RW_WIDTH), F32)]
               + [pltpu.VMEM((NS, RW_PAIRS, LANES, LANES), F32)])
    mixed, hg, rw, sh = pl.pallas_call(
        functools.partial(_mixer_kernel, C=C, NS=NS, stagger=STAGGER),
        grid=(B // NS, nc),
        in_specs=in_specs,
        out_specs=out_specs,
        out_shape=out_shape,
        scratch_shapes=scratch,
        compiler_params=pltpu.CompilerParams(
            dimension_semantics=("arbitrary", "arbitrary"), vmem_limit_bytes=VMEM_LIMIT),
        name="mixer",
    )(proj.reshape(NBLK, B, T, LANES), hg0, rw0, sh0, lb, hgn, mu, w0, a0, w2a, kk, ka, rk, lnw, lnb, wsel)
    return mixed.reshape(B * T, D_MODEL), hg, rw, sh


def _outproj_kernel(m_ref, w_ref, g_ref, x_ref, o_ref):
    z = jnp.dot(m_ref[...], w_ref[...], preferred_element_type=F32)
    ms = jnp.mean(z * z, axis=-1, keepdims=True)
    o_ref[...] = x_ref[...] + z * lax.rsqrt(ms + NORM_EPS) * g_ref[...]


def _outproj(mixed, w_bf, g, x, tm):
    rows = x.shape[0]
    return pl.pallas_call(
        _outproj_kernel,
        grid=(rows // tm,),
        in_specs=[
            pl.BlockSpec((tm, D_MODEL), lambda i: (i, 0)),
            pl.BlockSpec((D_MODEL, D_MODEL), lambda i: (0, 0)),
            pl.BlockSpec((1, D_MODEL), lambda i: (0, 0)),
            pl.BlockSpec((tm, D_MODEL), lambda i: (i, 0)),
        ],
        out_specs=pl.BlockSpec((tm, D_MODEL), lambda i: (i, 0)),
        out_shape=jax.ShapeDtypeStruct((rows, D_MODEL), F32),
        compiler_params=pltpu.CompilerParams(
            dimension_semantics=("arbitrary",), vmem_limit_bytes=VMEM_LIMIT),
        name="outproj",
    )(mixed, w_bf, g, x)


def _pick_tile(rows, cap):
    t = cap
    while rows % t:
        t //= 2
    return t


def _trunk(x, hg0, rw0, sh0, weights, C, need_y=True):
    (norm_pre, w_in_bf, mixer_params, w_out_bf, norm_post) = weights
    B, T, _ = x.shape
    x2 = x.reshape(B * T, D_MODEL)
    proj = _inproj(x2, norm_pre, w_in_bf, _pick_tile(B * T, INPROJ_ROWS))
    mixed, hg, rw, sh = _mixer(proj, hg0, rw0, sh0, mixer_params, B, T, C)
    y = None
    if need_y:
        y = _outproj(mixed, w_out_bf, norm_post, x2, _pick_tile(B * T, OUTPROJ_ROWS)).reshape(B, T, D_MODEL)
    return y, hg, rw, sh


def kernel(x_prompt, x_sample, state_hgrn, state_rwkv, state_shift, meta_tokens, norm_pre, w_in,
           hg_lower_bounds, hg_norm, rw_mu, rw_w0, rw_w2, rw_a0, rw_a2, rw_k_k, rw_k_a, rw_r_k,
           rw_ln_w, rw_ln_b, w_out, norm_post):
    B = x_prompt.shape[0]
    Bs = x_sample.shape[0]
    n_meta = meta_tokens.shape[0]
    l = 0

    lbs = jnp.cumsum(jax.nn.softmax(hg_lower_bounds.astype(F32), axis=0), axis=0)[l].reshape(1, HG_WIDTH)
    z64 = jnp.zeros((RW_HEAD, RW_WIDTH), F32)
    w2a = jnp.concatenate([jnp.concatenate([rw_w2[l], z64], axis=1),
                           jnp.concatenate([z64, rw_a2[l]], axis=1)], axis=0).astype(BF16)
    row = lambda t: t.reshape(1, -1).astype(F32)
    mixer_params = (lbs, row(hg_norm[l]), rw_mu[l].reshape(RW_BLK, 1, LANES), row(rw_w0[l]), row(rw_a0[l]),
                    w2a, row(rw_k_k[l]), row(rw_k_a[l]), row(rw_r_k[l]), row(rw_ln_w[l]), row(rw_ln_b[l]))
    zero_hg = jnp.zeros((1, HG_HEADS, LANES, LANES), F32)
    zero_rw = jnp.zeros((1, 2 * RW_PAIRS, RW_HEAD, RW_HEAD), F32)
    zero_sh = jnp.zeros((1, RW_BLK, 1, LANES), F32)
    proj_m, w_in_bf = _inproj_cast(meta_tokens.astype(F32), row(norm_pre[l]), w_in[l])
    _, hg_m, rw_m, sh_m = _mixer(proj_m, zero_hg, zero_rw, zero_sh, mixer_params, 1, n_meta, n_meta)
    weights = (row(norm_pre[l]), w_in_bf, mixer_params, w_out[l].astype(BF16), row(norm_post[l]))

    rep = lambda t: jnp.broadcast_to(t, (B,) + t.shape[1:])
    y_p, hg_p, rw_p, sh_p = _trunk(x_prompt, rep(hg_m), rep(rw_m), rep(sh_m), weights, C=CHUNK)

    sh_s0 = state_shift[l].reshape(Bs, RW_BLK, 1, LANES)
    y_s, hg_s, rw_s, sh_s = _trunk(x_sample, state_hgrn[l], state_rwkv[l], sh_s0, weights, C=CHUNK)

    unsh = lambda t: t.reshape(t.shape[0], 1, RW_BLK * LANES)[None]
    return (y_p, y_s,
            hg_p[None], rw_p[None], unsh(sh_p),
            hg_s[None], rw_s[None], unsh(sh_s))
```
